```python
import jax, jax.numpy as jnp
from jax import lax
import numpy as np

D_MODEL = 2048
BATCH = 1
SEQ = 8192
DEPTH = 4

N_MIXERS = 4
N_CONV = (DEPTH + 3) // 4
N_MLA = (DEPTH + 2) // 4
N_FNET = (DEPTH + 1) // 4
N_POOL = DEPTH // 4
EPS = 1e-6

CONV_WIDTH = 31
CONV_PAD = CONV_WIDTH // 2

N_HEADS = 16
Q_LORA = 512
KV_LORA = 512
QK_NOPE_DIM = 128
QK_ROPE_DIM = 64
V_HEAD_DIM = 128
QK_HEAD_DIM = QK_NOPE_DIM + QK_ROPE_DIM
ROPE_THETA = 10000.0
Q_BLOCK = 128

FNET_GROUPS = 4
FNET_GROUP_DIM = D_MODEL // FNET_GROUPS

POOL_WINDOWS = (2, 4, 8, 16)
POOL_GROUPS = len(POOL_WINDOWS)
POOL_GROUP_DIM = D_MODEL // POOL_GROUPS

N_EXPERTS = 16
N_EXPERT_GROUPS = 4
EXPERTS_PER_GROUP = N_EXPERTS // N_EXPERT_GROUPS
TOP_K = 2
EXPERT_FF = 512

kernel_name = 'hybrid_conv_mla_fnet_pool_grouped_moe_encoder'


def rms_norm(x, g):
    xf = x.astype(jnp.float32)
    y = xf * lax.rsqrt(jnp.mean(xf * xf, axis=-1, keepdims=True) + EPS)
    return (y * g.astype(jnp.float32)).astype(x.dtype)


def layer_norm(x, g, b):
    xf = x.astype(jnp.float32)
    mu = jnp.mean(xf, axis=-1, keepdims=True)
    xc = xf - mu
    y = xc * lax.rsqrt(jnp.mean(xc * xc, axis=-1, keepdims=True) + EPS)
    return (y * g.astype(jnp.float32) + b.astype(jnp.float32)).astype(x.dtype)


def rope_tables(positions):
    inv = ROPE_THETA ** (-jnp.arange(0, QK_ROPE_DIM, 2, dtype=jnp.float32) / QK_ROPE_DIM)
    ang = positions.astype(jnp.float32)[..., None] * inv
    return jnp.cos(ang)[:, :, None, :], jnp.sin(ang)[:, :, None, :]


def apply_rope(x, cos, sin):
    xf = x.astype(jnp.float32)
    x1, x2 = xf[..., :QK_ROPE_DIM // 2], xf[..., QK_ROPE_DIM // 2:]
    return jnp.concatenate([x1 * cos - x2 * sin, x2 * cos + x1 * sin], axis=-1).astype(x.dtype)


def conformer_conv(h, w1, b1, wdw, bdw, ln_g, ln_b, w2, b2):
    a = h @ w1 + b1
    u = a[..., :D_MODEL] * jax.nn.sigmoid(a[..., D_MODEL:])
    d = lax.conv_general_dilated(u, wdw, window_strides=(1,), padding=[(CONV_PAD, CONV_PAD)],
                                 dimension_numbers=('NWC', 'WIO', 'NWC'),
                                 feature_group_count=D_MODEL) + bdw
    d = jax.nn.silu(layer_norm(d, ln_g, ln_b))
    return d @ w2 + b2


def mla_attention(h, cos, sin, w_down, g_q, g_kv, w_uq, w_ukv, w_o):
    B, S, _ = h.shape
    down = h @ w_down
    c_q = rms_norm(down[..., :Q_LORA], g_q)
    c_kv = rms_norm(down[..., Q_LORA:Q_LORA + KV_LORA], g_kv)
    k_pe = down[..., Q_LORA + KV_LORA:]
    q = jnp.einsum('bsr,rhd->bshd', c_q, w_uq)
    kv = jnp.einsum('bsr,rhd->bshd', c_kv, w_ukv)
    q_nope, q_pe = q[..., :QK_NOPE_DIM], apply_rope(q[..., QK_NOPE_DIM:], cos, sin)
    k_nope, v = kv[..., :QK_NOPE_DIM], kv[..., QK_NOPE_DIM:]
    k_pe = apply_rope(k_pe[:, :, None, :], cos, sin)[:, :, 0, :]
    scale = QK_HEAD_DIM ** -0.5
    n_blk = S // Q_BLOCK

    def attend(blk):
        qn, qp = blk
        s = (jnp.einsum('bqhd,bkhd->bhqk', qn, k_nope)
             + jnp.einsum('bqhd,bkd->bhqk', qp, k_pe)).astype(jnp.float32) * scale
        p = jax.nn.softmax(s, axis=-1).astype(v.dtype)
        return jnp.einsum('bhqk,bkhd->bqhd', p, v)

    to_blocks = lambda t: t.reshape(B, n_blk, Q_BLOCK, *t.shape[2:]).swapaxes(0, 1)
    o = lax.map(attend, (to_blocks(q_nope), to_blocks(q_pe)))
    o = o.swapaxes(0, 1).reshape(B, S, N_HEADS, V_HEAD_DIM)
    return jnp.einsum('bshd,hde->bse', o, w_o)


def fourier_mix(h, w, b):
    B, S, _ = h.shape
    hf = h.astype(jnp.float32).reshape(B, S, FNET_GROUPS, FNET_GROUP_DIM)
    f = jnp.fft.fft2(hf, axes=(1, 3), norm='ortho').real
    return f.astype(h.dtype).reshape(B, S, D_MODEL) @ w + b


def pool_mix(h, w_pool, scale):
    B, S, _ = h.shape
    hf = h.astype(jnp.float32).reshape(B, S, POOL_GROUPS, POOL_GROUP_DIM)
    cs = jnp.concatenate([jnp.zeros((B, 1, POOL_GROUPS, POOL_GROUP_DIM), jnp.float32),
                          jnp.cumsum(hf, axis=1)], axis=1)
    t = np.arange(S)[None, :]
    win = np.array(POOL_WINDOWS)[:, None]
    lo = np.clip(t - win // 2, 0, S)
    hi = np.clip(t + win - win // 2, 0, S)
    gi = np.arange(POOL_GROUPS)[None, :]
    wsum = cs[:, jnp.asarray(hi.T), gi] - cs[:, jnp.asarray(lo.T), gi]
    count = jnp.asarray((hi - lo).T.astype(np.float32))
    mixed = wsum / count[None, :, :, None] - hf
    y = jnp.einsum('bsgc,gce->bsge', mixed.astype(h.dtype), w_pool).reshape(B, S, D_MODEL)
    return y * scale


def grouped_moe(h, router_w, router_b, w_gate, w_up, w_down):
    B, S, _ = h.shape
    xt = h.reshape(-1, D_MODEL)
    s = jax.nn.sigmoid((xt @ router_w).astype(jnp.float32))
    sel = s + router_b.astype(jnp.float32)
    selg = sel.reshape(-1, N_EXPERT_GROUPS, EXPERTS_PER_GROUP)
    gscore = lax.top_k(selg, TOP_K)[0].sum(-1)
    gidx = jnp.argmax(gscore, axis=-1)
    sel_in = jnp.take_along_axis(selg, gidx[:, None, None], axis=1)[:, 0]
    _, loc = lax.top_k(sel_in, TOP_K)
    eidx = gidx[:, None] * EXPERTS_PER_GROUP + loc
    wts = jnp.take_along_axis(s, eidx, axis=-1)
    wts = wts / jnp.sum(wts, axis=-1, keepdims=True)
    combine = jnp.einsum('tk,tke->te', wts, jax.nn.one_hot(eidx, N_EXPERTS, dtype=jnp.float32)).astype(h.dtype)
    g = jnp.einsum('td,edf->tef', xt, w_gate)
    u = jnp.einsum('td,edf->tef', xt, w_up)
    act = jax.nn.silu(g) * u * combine[:, :, None]
    return jnp.einsum('tef,efd->td', act, w_down).reshape(B, S, D_MODEL)


def _normal(k, shape, scale):
    return scale * jax.random.normal(k, shape, jnp.float32)


def setup_inputs(seed: int = 0) -> dict:
    key = jax.random.key(seed)
    ks = jax.random.split(key, 32)
    D = D_MODEL
    pos = jnp.arange(SEQ, dtype=jnp.int32)[None, :] + jax.random.randint(ks[2], (BATCH, 1), 0, 1024, dtype=jnp.int32)
    return {
        'x': _normal(ks[0], (BATCH, SEQ, D), 1.0),
        'c': _normal(ks[1], (BATCH, D), 1.0),
        'positions': pos,
        'ada_w': _normal(ks[3], (DEPTH, D, 6 * D), 0.5 * D ** -0.5),
        'ada_b': _normal(ks[4], (DEPTH, 6 * D), 0.02),
        'norm_g': 1.0 + _normal(ks[5], (DEPTH, 2, D), 0.02),
        'final_g': 1.0 + _normal(ks[6], (D,), 0.02),
        'conv_w1': _normal(ks[7], (N_CONV, D, 2 * D), D ** -0.5),
        'conv_b1': _normal(ks[8], (N_CONV, 2 * D), 0.02),
        'conv_wdw': _normal(ks[9], (N_CONV, CONV_WIDTH, 1, D), CONV_WIDTH ** -0.5),
        'conv_bdw': _normal(ks[10], (N_CONV, D), 0.02),
        'conv_ln_g': 1.0 + _normal(ks[11], (N_CONV, D), 0.02),
        'conv_ln_b': _normal(ks[12], (N_CONV, D), 0.02),
        'conv_w2': _normal(ks[13], (N_CONV, D, D), D ** -0.5),
        'conv_b2': _normal(ks[14], (N_CONV, D), 0.02),
        'mla_w_down': _normal(ks[15], (N_MLA, D, Q_LORA + KV_LORA + QK_ROPE_DIM), D ** -0.5),
        'mla_g_q': 1.0 + _normal(ks[16], (N_MLA, Q_LORA), 0.02),
        'mla_g_kv': 1.0 + _normal(ks[17], (N_MLA, KV_LORA), 0.02),
        'mla_w_uq': _normal(ks[18], (N_MLA, Q_LORA, N_HEADS, QK_HEAD_DIM), Q_LORA ** -0.5),
        'mla_w_ukv': _normal(ks[19], (N_MLA, KV_LORA, N_HEADS, QK_NOPE_DIM + V_HEAD_DIM), KV_LORA ** -0.5),
        'mla_w_o': _normal(ks[20], (N_MLA, N_HEADS, V_HEAD_DIM, D), (N_HEADS * V_HEAD_DIM) ** -0.5),
        'fnet_w': _normal(ks[21], (N_FNET, D, D), D ** -0.5),
        'fnet_b': _normal(ks[22], (N_FNET, D), 0.02),
        'pool_w': _normal(ks[23], (N_POOL, POOL_GROUPS, POOL_GROUP_DIM, POOL_GROUP_DIM), POOL_GROUP_DIM ** -0.5),
        'pool_scale': 1.0 + _normal(ks[24], (N_POOL, D), 0.1),
        'router_w': _normal(ks[25], (D, N_EXPERTS), D ** -0.5),
        'router_b': _normal(ks[26], (N_EXPERTS,), 0.01),
        'moe_w_gate': _normal(ks[27], (DEPTH, N_EXPERTS, D, EXPERT_FF), D ** -0.5),
        'moe_w_up': _normal(ks[28], (DEPTH, N_EXPERTS, D, EXPERT_FF), D ** -0.5),
        'moe_w_down': _normal(ks[29], (DEPTH, N_EXPERTS, EXPERT_FF, D), EXPERT_FF ** -0.5),
    }


def reference(x, c, positions, ada_w, ada_b, norm_g, final_g,
              conv_w1, conv_b1, conv_wdw, conv_bdw, conv_ln_g, conv_ln_b, conv_w2, conv_b2,
              mla_w_down, mla_g_q, mla_g_kv, mla_w_uq, mla_w_ukv, mla_w_o,
              fnet_w, fnet_b, pool_w, pool_scale,
              router_w, router_b, moe_w_gate, moe_w_up, moe_w_down):
    cos, sin = rope_tables(positions)
    c_act = jax.nn.silu(c)
    for i in range(DEPTH):
        mod = c_act @ ada_w[i] + ada_b[i]
        sh1, sc1, g1, sh2, sc2, g2 = [m[:, None, :] for m in jnp.split(mod, 6, axis=-1)]
        h = rms_norm(x, norm_g[i, 0]) * (1.0 + sc1) + sh1
        kind, j = i % N_MIXERS, i // N_MIXERS
        if kind == 0:
            y = conformer_conv(h, conv_w1[j], conv_b1[j], conv_wdw[j], conv_bdw[j],
                               conv_ln_g[j], conv_ln_b[j], conv_w2[j], conv_b2[j])
        elif kind == 1:
            y = mla_attention(h, cos, sin, mla_w_down[j], mla_g_q[j], mla_g_kv[j],
                              mla_w_uq[j], mla_w_ukv[j], mla_w_o[j])
        elif kind == 2:
            y = fourier_mix(h, fnet_w[j], fnet_b[j])
        else:
            y = pool_mix(h, pool_w[j], pool_scale[j])
        x = x + g1 * y
        h = rms_norm(x, norm_g[i, 1]) * (1.0 + sc2) + sh2
        x = x + g2 * grouped_moe(h, router_w, router_b, moe_w_gate[i], moe_w_up[i], moe_w_down[i])
    return rms_norm(x, final_g)
```

```python
import functools
import math

import numpy as np
import jax
import jax.numpy as jnp
from jax import lax
from jax.experimental import pallas as pl
from jax.experimental.pallas import tpu as pltpu

F32 = jnp.float32
BF16 = jnp.bfloat16
I32 = jnp.int32

D_MODEL = 2048
DEPTH = 4
EPS = 1e-6
CONV_WIDTH = 31
CONV_PAD = CONV_WIDTH // 2
N_HEADS = 16
Q_LORA = 512
KV_LORA = 512
QK_NOPE_DIM = 128
QK_ROPE_DIM = 64
V_HEAD_DIM = 128
QK_HEAD_DIM = QK_NOPE_DIM + QK_ROPE_DIM
ROPE_THETA = 10000.0
FNET_GROUPS = 4
FNET_GROUP_DIM = D_MODEL // FNET_GROUPS
POOL_WINDOWS = (2, 4, 8, 16)
POOL_GROUP_DIM = D_MODEL // len(POOL_WINDOWS)
N_EXPERTS = 16
N_EXPERT_GROUPS = 4
EXPERTS_PER_GROUP = 4
EXPERT_FF = 512

LANES = 128
HEAD_PAD = 256
DFT_N1 = 128
MOE_CHUNK = 512
MOE_FF_SPLIT = 2
MIB = 1 << 20


def _cparams(sem, vmem_mib):
    return pltpu.CompilerParams(dimension_semantics=sem, vmem_limit_bytes=vmem_mib * MIB)


def _full(shape):
    nd = len(shape)
    return pl.BlockSpec(shape, lambda *_: (0,) * nd)


def _norm_mod_rows(x, g, sp, sh):
    ms = jnp.mean(x * x, axis=-1, keepdims=True)
    y = x * lax.rsqrt(ms + EPS)
    return (y * g) * sp + sh


def _norm_mod_store(x_ref, g_ref, sc_ref, sh_ref, out_ref, rows, chunk=32):
    g = g_ref[...]
    sp = 1.0 + sc_ref[...]
    sh = sh_ref[...]

    def body(c, carry):
        r = pl.multiple_of(c * chunk, chunk)
        x = x_ref[pl.ds(r, chunk), :]
        out_ref[pl.ds(r, chunk), :] = _norm_mod_rows(x, g, sp, sh).astype(out_ref.dtype)
        return carry

    lax.fori_loop(0, rows // chunk, body, 0)


def _mod_body(c_ref, w_ref, b_ref, o_ref):
    k_dim = c_ref.shape[0]
    tn = o_ref.shape[-1]
    rc = 16

    def body(k, acc):
        r = pl.multiple_of(k * rc, rc)
        c = c_ref[pl.ds(r, rc), :]
        ca = c * jax.nn.sigmoid(c)
        return acc + w_ref[0, pl.ds(r, rc), :] * ca

    acc = lax.fori_loop(0, k_dim // rc, body, jnp.zeros((rc, tn), F32), unroll=4)
    o_ref[0] = jnp.sum(acc, axis=0, keepdims=True) + b_ref[0]


def _ada_mod(c, ada_w, ada_b):
    depth, d, n = ada_w.shape
    tn = 1024
    return pl.pallas_call(
        _mod_body,
        out_shape=jax.ShapeDtypeStruct((depth, 1, n), F32),
        grid=(depth, n // tn),
        in_specs=[
            _full((d, 1)),
            pl.BlockSpec((1, d, tn), lambda i, j: (i, 0, j)),
            pl.BlockSpec((1, 1, tn), lambda i, j: (i, 0, j)),
        ],
        out_specs=pl.BlockSpec((1, 1, tn), lambda i, j: (i, 0, j)),
        compiler_params=_cparams(("arbitrary", "arbitrary"), 40),
        name="ada_mod",
    )(c.reshape(d, 1), ada_w, ada_b.reshape(depth, 1, n))


def _mm_res_body(a_ref, w_ref, b_ref, x_ref, g_ref, o_ref):
    y = jnp.dot(a_ref[...], w_ref[...].astype(BF16), preferred_element_type=F32) + b_ref[...]
    o_ref[...] = x_ref[...] + g_ref[...] * y


def _mm_res(a, w, b, x, gate, tm=1024, tn=512):
    m, k = a.shape
    n = w.shape[1]
    return pl.pallas_call(
        _mm_res_body,
        out_shape=jax.ShapeDtypeStruct((m, n), F32),
        grid=(m // tm, n // tn),
        in_specs=[
            pl.BlockSpec((tm, k), lambda i, j: (i, 0)),
            pl.BlockSpec((k, tn), lambda i, j: (0, j)),
            pl.BlockSpec((1, tn), lambda i, j: (0, j)),
            pl.BlockSpec((tm, tn), lambda i, j: (i, j)),
            pl.BlockSpec((1, tn), lambda i, j: (0, j)),
        ],
        out_specs=pl.BlockSpec((tm, tn), lambda i, j: (i, j)),
        compiler_params=_cparams(("arbitrary", "arbitrary"), 48),
        name="mm_res",
    )(a, w, b, x, gate)


def _conv1_body(x_ref, g_ref, sc_ref, sh_ref, wa_ref, wb_ref, ba_ref, bb_ref, u_ref, h_scr):
    @pl.when(pl.program_id(1) == 0)
    def _():
        _norm_mod_store(x_ref, g_ref, sc_ref, sh_ref, h_scr, h_scr.shape[0])

    h = h_scr[...]
    a = jnp.dot(h, wa_ref[...].astype(BF16), preferred_element_type=F32) + ba_ref[...]
    b = jnp.dot(h, wb_ref[...].astype(BF16), preferred_element_type=F32) + bb_ref[...]
    u_ref[...] = a * jax.nn.sigmoid(b)


def _conv_glu(x, g, sc, sh, w1, b1, tm=1024, tn=512):
    s, d = x.shape
    nb = d // tn
    b1r = b1.reshape(1, 2 * d)
    row = _full((1, d))
    return pl.pallas_call(
        _conv1_body,
        out_shape=jax.ShapeDtypeStruct((s, d), F32),
        grid=(s // tm, nb),
        in_specs=[
            pl.BlockSpec((tm, d), lambda i, j: (i, 0)),
            row, row, row,
            pl.BlockSpec((d, tn), lambda i, j: (0, j)),
            pl.BlockSpec((d, tn), lambda i, j: (0, j + nb)),
            pl.BlockSpec((1, tn), lambda i, j: (0, j)),
            pl.BlockSpec((1, tn), lambda i, j: (0, j + nb)),
        ],
        out_specs=pl.BlockSpec((tm, tn), lambda i, j: (i, j)),
        scratch_shapes=[pltpu.VMEM((tm, d), BF16)],
        compiler_params=_cparams(("arbitrary", "arbitrary"), 56),
        name="conv_glu",
    )(x, g, sc, sh, w1, w1, b1r, b1r)


def _dwconv_body(up_ref, u_ref, un_ref, w_ref, bdw_ref, lg_ref, lb_ref, d_ref, buf, cv, *, ts, cw, ncw):
    i = pl.program_id(0)
    cj = pl.program_id(1)
    ni = pl.num_programs(0)
    halo = 16
    buf[0:halo, :] = jnp.where(i > 0, up_ref[...], 0.0)
    buf[halo:halo + ts, :] = u_ref[...]
    buf[halo + ts:2 * halo + ts, :] = jnp.where(i < ni - 1, un_ref[...], 0.0)
    rc = 32
    for lc in range(cw // LANES):
        ls = slice(lc * LANES, (lc + 1) * LANES)
        wcol = w_ref[:, ls]
        bcol = bdw_ref[:, ls]

        def body(r, carry, ls=ls, wcol=wcol, bcol=bcol):
            r0 = pl.multiple_of(r * rc, rc)
            wrows = rc + 2 * halo
            win = buf[pl.ds(r0, wrows), ls]
            acc = jnp.zeros((rc, LANES), F32)
            for phase in range(8):
                shifted = win if phase == 0 else pltpu.roll(win, wrows - phase, 0)
                for k in range(CONV_WIDTH):
                    off = halo - CONV_PAD + k
                    if off % 8 == phase:
                        acc = acc + shifted[off - phase:off - phase + rc, :] * wcol[k:k + 1, :]
            cv[cj, pl.ds(r0, rc), ls] = acc + bcol
            return carry

        lax.fori_loop(0, ts // rc, body, 0)

    @pl.when(cj == ncw - 1)
    def _():
        d_model = ncw * cw
        rc2 = 32

        def body2(r, carry):
            r0 = pl.multiple_of(r * rc2, rc2)
            parts = [cv[c, pl.ds(r0, rc2), :] for c in range(ncw)]
            tot = parts[0].sum(axis=-1, keepdims=True)
            for p in parts[1:]:
                tot = tot + p.sum(axis=-1, keepdims=True)
            mu = tot / d_model
            cen = [p - mu for p in parts]
            sq = (cen[0] * cen[0]).sum(axis=-1, keepdims=True)
            for p in cen[1:]:
                sq = sq + (p * p).sum(axis=-1, keepdims=True)
            rinv = lax.rsqrt(sq / d_model + EPS)
            for c in range(ncw):
                cs = slice(c * cw, (c + 1) * cw)
                y = (cen[c] * rinv) * lg_ref[:, cs] + lb_ref[:, cs]
                d_ref[pl.ds(r0, rc2), cs] = (y * jax.nn.sigmoid(y)).astype(d_ref.dtype)
            return carry

        lax.fori_loop(0, ts // rc2, body2, 0)


def _dwconv_ln_silu(u, wdw, bdw, ln_g, ln_b, ts=512, cw=512):
    s, d = u.shape
    ncw = d // cw
    hb = ts // 16
    nhb = s // 16
    body = functools.partial(_dwconv_body, ts=ts, cw=cw, ncw=ncw)
    return pl.pallas_call(
        body,
        out_shape=jax.ShapeDtypeStruct((s, d), BF16),
        grid=(s // ts, ncw),
        in_specs=[
            pl.BlockSpec((16, cw), lambda i, j: (jnp.maximum(i * hb - 1, 0), j)),
            pl.BlockSpec((ts, cw), lambda i, j: (i, j)),
            pl.BlockSpec((16, cw), lambda i, j: (jnp.minimum((i + 1) * hb, nhb - 1), j)),
            pl.BlockSpec((CONV_WIDTH, cw), lambda i, j: (0, j)),
            pl.BlockSpec((1, cw), lambda i, j: (0, j)),
            _full((1, d)),
            _full((1, d)),
        ],
        out_specs=pl.BlockSpec((ts, d), lambda i, j: (i, 0)),
        scratch_shapes=[pltpu.VMEM((ts + 32, cw), F32), pltpu.VMEM((ncw, ts, cw), F32)],
        compiler_params=_cparams(("arbitrary", "arbitrary"), 32),
        name="dwconv_ln_silu",
    )(u, u, u, wdw, bdw, ln_g, ln_b)


def _rope_rot(t, cz, sz):
    return t * cz + pltpu.roll(t, QK_ROPE_DIM, 1) * sz


def _mla_down_body(x_ref, g_ref, sc_ref, sh_ref, w_ref, gq_ref, gkv_ref, pos_ref, rc_ref,
                   cq_ref, ckv_ref, kpe_ref, cz_ref, sz_ref, h_scr, w_scr):
    @pl.when(pl.program_id(0) == 0)
    def _():
        w_scr[...] = w_ref[...].astype(BF16)

    _norm_mod_store(x_ref, g_ref, sc_ref, sh_ref, h_scr, h_scr.shape[0])
    down = jnp.dot(h_scr[...], w_scr[...], preferred_element_type=F32)
    cq = down[:, :Q_LORA]
    ckv = down[:, Q_LORA:Q_LORA + KV_LORA]
    cq_ref[...] = (cq * lax.rsqrt(jnp.mean(cq * cq, axis=-1, keepdims=True) + EPS) * gq_ref[...]).astype(BF16)
    ckv_ref[...] = (ckv * lax.rsqrt(jnp.mean(ckv * ckv, axis=-1, keepdims=True) + EPS) * gkv_ref[...]).astype(BF16)
    ang = pos_ref[...].astype(F32) * rc_ref[0:1, :]
    cz = jnp.cos(ang) * rc_ref[1:2, :]
    sz = jnp.sin(ang) * rc_ref[2:3, :]
    cz_ref[...] = cz
    sz_ref[...] = sz
    kpe_ref[...] = _rope_rot(down[:, Q_LORA + KV_LORA:], cz, sz).astype(BF16)


def _mla_down(x, g, sc, sh, w_ext, gq, gkv, pos_col, rope_c, tm=512):
    s, d = x.shape
    n = w_ext.shape[1]
    row = _full((1, d))
    return pl.pallas_call(
        _mla_down_body,
        out_shape=(
            jax.ShapeDtypeStruct((s, Q_LORA), BF16),
            jax.ShapeDtypeStruct((s, KV_LORA), BF16),
            jax.ShapeDtypeStruct((s, LANES), BF16),
            jax.ShapeDtypeStruct((s, LANES), F32),
            jax.ShapeDtypeStruct((s, LANES), F32),
        ),
        grid=(s // tm,),
        in_specs=[
            pl.BlockSpec((tm, d), lambda i: (i, 0)),
            row, row, row,
            _full((d, n)),
            _full((1, Q_LORA)),
            _full((1, KV_LORA)),
            pl.BlockSpec((tm, 1), lambda i: (i, 0)),
            _full((8, LANES)),
        ],
        out_specs=(
            pl.BlockSpec((tm, Q_LORA), lambda i: (i, 0)),
            pl.BlockSpec((tm, KV_LORA), lambda i: (i, 0)),
            pl.BlockSpec((tm, LANES), lambda i: (i, 0)),
            pl.BlockSpec((tm, LANES), lambda i: (i, 0)),
            pl.BlockSpec((tm, LANES), lambda i: (i, 0)),
        ),
        scratch_shapes=[pltpu.VMEM((tm, d), BF16), pltpu.VMEM((d, n), BF16)],
        compiler_params=_cparams(("arbitrary",), 56),
        name="mla_down",
    )(x, g, sc, sh, w_ext, gq, gkv, pos_col, rope_c)


def _mla_up_body(cq_ref, ckv_ref, kpe_ref, cz_ref, sz_ref, wq_ref, wkv_ref, q_ref, k_ref, v_ref, *, qscale):
    qf = jnp.dot(cq_ref[...], wq_ref[0].astype(BF16), preferred_element_type=F32)
    qpe = _rope_rot(qf[:, QK_NOPE_DIM:], cz_ref[...], sz_ref[...])
    q_ref[0] = (jnp.concatenate([qf[:, :QK_NOPE_DIM], qpe], axis=1) * qscale).astype(BF16)
    kv = jnp.dot(ckv_ref[...], wkv_ref[...].astype(BF16), preferred_element_type=F32)
    k_ref[0] = jnp.concatenate([kv[:, :QK_NOPE_DIM].astype(BF16), kpe_ref[...]], axis=1)
    v = kv[:, QK_NOPE_DIM:]
    v_ref[0] = jnp.concatenate([v, jnp.ones_like(v)], axis=1).astype(BF16)


def _mla_up(cq, ckv, kpe, cz, sz, wq_ext, wkv2d, tm=1024):
    s = cq.shape[0]
    qscale = (QK_HEAD_DIM ** -0.5) * math.log2(math.e)
    body = functools.partial(_mla_up_body, qscale=qscale)
    hs = jax.ShapeDtypeStruct((N_HEADS, s, HEAD_PAD), BF16)
    hspec = pl.BlockSpec((1, tm, HEAD_PAD), lambda i, h: (h, i, 0))
    return pl.pallas_call(
        body,
        out_shape=(hs, hs, hs),
        grid=(s // tm, N_HEADS),
        in_specs=[
            pl.BlockSpec((tm, Q_LORA), lambda i, h: (i, 0)),
            pl.BlockSpec((tm, KV_LORA), lambda i, h: (i, 0)),
            pl.BlockSpec((tm, LANES), lambda i, h: (i, 0)),
            pl.BlockSpec((tm, LANES), lambda i, h: (i, 0)),
            pl.BlockSpec((tm, LANES), lambda i, h: (i, 0)),
            pl.BlockSpec((1, Q_LORA, HEAD_PAD), lambda i, h: (h, 0, 0)),
            pl.BlockSpec((KV_LORA, HEAD_PAD), lambda i, h: (0, h)),
        ],
        out_specs=(hspec, hspec, hspec),
        compiler_params=_cparams(("arbitrary", "arbitrary"), 32),
        name="mla_up",
    )(cq, ckv, kpe, cz, sz, wq_ext, wkv2d)


def _attn_body(q_ref, k_ref, v_ref, o_ref, m_scr, acc_scr, *, tk, nk):
    q = q_ref[0]
    m_scr[...] = jnp.full(m_scr.shape, -jnp.inf, F32)
    acc_scr[...] = jnp.zeros(acc_scr.shape, F32)

    def body(c, carry):
        r = pl.multiple_of(c * tk, tk)
        k = k_ref[0, pl.ds(r, tk), :]
        v = v_ref[0, pl.ds(r, tk), :]
        s = lax.dot_general(q, k, (((1,), (1,)), ((), ())), preferred_element_type=F32)
        m_old = m_scr[...]
        m_new = jnp.maximum(m_old, jnp.max(s, axis=-1, keepdims=True))
        alpha = jnp.exp2(m_old - m_new)
        p = jnp.exp2(s - m_new).astype(BF16)
        acc_scr[...] = acc_scr[...] * alpha + jnp.dot(p, v, preferred_element_type=F32)
        m_scr[...] = m_new
        return carry

    lax.fori_loop(0, nk, body, 0)
    acc = acc_scr[...]
    o_ref[...] = (acc[:, :V_HEAD_DIM] / acc[:, V_HEAD_DIM:V_HEAD_DIM + 1]).astype(o_ref.dtype)


def _attention(q, k, v, tq=512, tk=1024):
    nh, s, _ = q.shape
    tk = min(tk, s)
    tq = min(tq, s)
    body = functools.partial(_attn_body, tk=tk, nk=s // tk)
    return pl.pallas_call(
        body,
        out_shape=jax.ShapeDtypeStruct((s, nh * V_HEAD_DIM), BF16),
        grid=(nh, s // tq),
        in_specs=[
            pl.BlockSpec((1, tq, HEAD_PAD), lambda h, i: (h, i, 0)),
            pl.BlockSpec((1, s, HEAD_PAD), lambda h, i: (h, 0, 0)),
            pl.BlockSpec((1, s, HEAD_PAD), lambda h, i: (h, 0, 0)),
        ],
        out_specs=pl.BlockSpec((tq, V_HEAD_DIM), lambda h, i: (i, h)),
        scratch_shapes=[pltpu.VMEM((tq, 1), F32), pltpu.VMEM((tq, HEAD_PAD), F32)],
        compiler_params=_cparams(("arbitrary", "arbitrary"), 40),
        name="attention",
    )(q, k, v)


def _dft_tables(s):
    n1 = DFT_N1
    n2 = s // n1
    m = FNET_GROUP_DIM
    c = np.arange(m, dtype=np.float64)
    ang = 2.0 * np.pi * np.outer(c, c) / m
    cs = np.concatenate([np.cos(ang), np.sin(ang)], axis=1)
    a1 = 2.0 * np.pi * np.outer(np.arange(n1), np.arange(n1)) / n1
    c1, s1 = np.cos(a1), np.sin(a1)
    m1 = np.block([[c1, -s1], [-s1, -c1]])
    at = 2.0 * np.pi * np.outer(np.arange(n2), np.arange(n1)) / s
    ct = np.cos(at)[:, :, None]
    st = np.sin(at)[:, :, None]
    a2 = 2.0 * np.pi * np.outer(np.arange(n2), np.arange(n2)) / n2
    scale = 1.0 / math.sqrt(float(s) * m)
    jb = 16
    gmat = np.zeros((n2, jb, 2, n2, jb), dtype=np.float64)
    for j in range(jb):
        gmat[:, j, 0, :, j] = np.cos(a2) * scale
        gmat[:, j, 1, :, j] = np.sin(a2) * scale
    gmat = gmat.reshape(n2 * jb, 2 * n2 * jb)
    return (jnp.asarray(cs, BF16), jnp.asarray(m1, BF16), jnp.asarray(ct, F32), jnp.asarray(st, F32),
            jnp.asarray(gmat, BF16))


def _fnet1_body(x_ref, g_ref, sc_ref, sh_ref, cs_ref, m1_ref, ct_ref, st_ref, o_ref, pq_scr, *, nb):
    n1 = DFT_N1
    g = g_ref[...]
    sp = 1.0 + sc_ref[...]
    sh = sh_ref[...]
    cs = cs_ref[...]
    m1 = m1_ref[...]
    gd = FNET_GROUP_DIM
    for j in range(nb):
        h = _norm_mod_rows(x_ref[:, j, :], g, sp, sh).astype(BF16)
        for gi in range(FNET_GROUPS):
            cols = slice(gi * gd, (gi + 1) * gd)
            pq = jnp.dot(h[:, cols], cs, preferred_element_type=F32)
            pq_scr[0:n1, cols] = pq[:, :gd].astype(BF16)
            pq_scr[n1:2 * n1, cols] = pq[:, gd:].astype(BF16)
        a = jnp.dot(m1, pq_scr[...], preferred_element_type=F32)
        ar = a[:n1]
        ai = a[n1:]
        ct = ct_ref[j]
        st = st_ref[j]
        o_ref[0, j] = (ar * ct + ai * st).astype(BF16)
        o_ref[1, j] = (ai * ct - ar * st).astype(BF16)


def _fnet_stage1(x, g, sc, sh, cs, m1, ct, st, nb=8):
    s, d = x.shape
    n1 = DFT_N1
    n2 = s // n1
    body = functools.partial(_fnet1_body, nb=nb)
    row = _full((1, d))
    return pl.pallas_call(
        body,
        out_shape=jax.ShapeDtypeStruct((2, n2, n1, d), BF16),
        grid=(n2 // nb,),
        in_specs=[
            pl.BlockSpec((n1, nb, d), lambda b: (0, b, 0)),
            row, row, row,
            _full(cs.shape),
            _full(m1.shape),
            pl.BlockSpec((nb, n1, 1), lambda b: (b, 0, 0)),
            pl.BlockSpec((nb, n1, 1), lambda b: (b, 0, 0)),
        ],
        out_specs=pl.BlockSpec((2, nb, n1, d), lambda b: (0, b, 0, 0)),
        scratch_shapes=[pltpu.VMEM((2 * n1, d), BF16)],
        compiler_params=_cparams(("arbitrary",), 48),
        name="fnet_stage1",
    )(x.reshape(n1, n2, d), g, sc, sh, cs, m1, ct, st)


def _fnet2_body(a_ref, g_ref, o_ref):
    blk = a_ref[...]
    rows = blk.shape[0] * blk.shape[1] * blk.shape[2]
    b2 = blk.reshape(rows, blk.shape[3])
    f = jnp.dot(g_ref[...], b2, preferred_element_type=F32)
    o_ref[...] = f.astype(o_ref.dtype).reshape(o_ref.shape)


def _fnet_stage2(a4, gmat, jb=16):
    _, n2, n1, d = a4.shape
    return pl.pallas_call(
        _fnet2_body,
        out_shape=jax.ShapeDtypeStruct((n2, n1, d), BF16),
        grid=(n1 // jb,),
        in_specs=[
            pl.BlockSpec((2, n2, jb, d), lambda b: (0, 0, b, 0)),
            _full(gmat.shape),
        ],
        out_specs=pl.BlockSpec((n2, jb, d), lambda b: (0, b, 0)),
        compiler_params=_cparams(("arbitrary",), 48),
        name="fnet_stage2",
    )(a4, gmat)


def _pool_body(xp_ref, x_ref, xn_ref, g_ref, sc_ref, sh_ref, w_ref, ps_ref, g1_ref, o_ref, hbuf, w_scr, *, ts, seq):
    i = pl.program_id(0)
    ni = pl.num_programs(0)
    halo = 8

    @pl.when(i == 0)
    def _():
        w_scr[...] = w_ref[...].astype(BF16)

    g = g_ref[...]
    sp = 1.0 + sc_ref[...]
    sh = sh_ref[...]
    hbuf[0:halo, :] = jnp.where(i > 0, _norm_mod_rows(xp_ref[...], g, sp, sh), 0.0)
    _norm_mod_store(x_ref, g_ref, sc_ref, sh_ref, hbuf.at[pl.ds(halo, ts), :], ts)
    hbuf[halo + ts:2 * halo + ts, :] = jnp.where(i < ni - 1, _norm_mod_rows(xn_ref[...], g, sp, sh), 0.0)

    t = i * ts + lax.broadcasted_iota(I32, (ts, 1), 0)
    gd = POOL_GROUP_DIM
    for gi, win in enumerate(POOL_WINDOWS):
        cols = slice(gi * gd, (gi + 1) * gd)
        half = win // 2
        wsum = hbuf[halo - half:halo - half + ts, cols]
        for off in range(-half + 1, win - half):
            wsum = wsum + hbuf[halo + off:halo + off + ts, cols]
        count = (jnp.minimum(t + (win - half), seq) - jnp.maximum(t - half, 0)).astype(F32)
        mixed = wsum / count - hbuf[halo:halo + ts, cols]
        y = jnp.dot(mixed.astype(BF16), w_scr[gi], preferred_element_type=F32)
        o_ref[:, cols] = x_ref[:, cols] + g1_ref[:, cols] * (y * ps_ref[:, cols])


def _pool_layer(x, g, sc, sh, w_pool, pscale, g1, ts=512):
    s, d = x.shape
    hb = ts // 8
    nhb = s // 8
    body = functools.partial(_pool_body, ts=ts, seq=s)
    row = _full((1, d))
    return pl.pallas_call(
        body,
        out_shape=jax.ShapeDtypeStruct((s, d), F32),
        grid=(s // ts,),
        in_specs=[
            pl.BlockSpec((8, d), lambda i: (jnp.maximum(i * hb - 1, 0), 0)),
            pl.BlockSpec((ts, d), lambda i: (i, 0)),
            pl.BlockSpec((8, d), lambda i: (jnp.minimum((i + 1) * hb, nhb - 1), 0)),
            row, row, row,
            _full(w_pool.shape),
            row, row,
        ],
        out_specs=pl.BlockSpec((ts, d), lambda i: (i, 0)),
        scratch_shapes=[pltpu.VMEM((ts + 16, d), F32), pltpu.VMEM(w_pool.shape, BF16)],
        compiler_params=_cparams(("arbitrary",), 48),
        name="pool_layer",
    )(x, x, x, g, sc, sh, w_pool, pscale, g1)


def _router_body(x_ref, g_ref, sc_ref, sh_ref, rw_ref, rb_ref, grp_ref, info_ref, h_scr):
    tm = h_scr.shape[0]
    _norm_mod_store(x_ref, g_ref, sc_ref, sh_ref, h_scr, tm)
    logits = lax.dot_general(rw_ref[...], h_scr[...], (((1,), (1,)), ((), ())),
                             precision=lax.Precision.HIGHEST, preferred_element_type=F32)
    sc = jax.nn.sigmoid(logits)
    sel = sc + rb_ref[...]
    epg = EXPERTS_PER_GROUP
    rows = [sel[e:e + 1, :] for e in range(N_EXPERTS)]
    srow = [sc[e:e + 1, :] for e in range(N_EXPERTS)]
    gscore = []
    for gi in range(N_EXPERT_GROUPS):
        r = rows[gi * epg:(gi + 1) * epg]
        best = None
        for a in range(epg):
            for b in range(a + 1, epg):
                ps = r[a] + r[b]
                best = ps if best is None else jnp.maximum(best, ps)
        gscore.append(best)
    gidx = jnp.zeros_like(gscore[0], dtype=I32)
    gbest = gscore[0]
    for gi in range(1, N_EXPERT_GROUPS):
        better = gscore[gi] > gbest
        gidx = jnp.where(better, gi, gidx)
        gbest = jnp.where(better, gscore[gi], gbest)
    vin = []
    sin_ = []
    for j in range(epg):
        v = rows[j]
        sv = srow[j]
        for gi in range(1, N_EXPERT_GROUPS):
            pick = gidx == gi
            v = jnp.where(pick, rows[gi * epg + j], v)
            sv = jnp.where(pick, srow[gi * epg + j], sv)
        vin.append(v)
        sin_.append(sv)
    l1 = jnp.zeros_like(gidx)
    b1 = vin[0]
    for j in range(1, epg):
        better = vin[j] > b1
        l1 = jnp.where(better, j, l1)
        b1 = jnp.where(better, vin[j], b1)
    neg = jnp.full_like(b1, -jnp.inf)
    l2 = jnp.zeros_like(gidx)
    b2 = neg
    for j in range(epg):
        vj = jnp.where(l1 == j, neg, vin[j])
        better = vj > b2
        l2 = jnp.where(better, j, l2)
        b2 = jnp.where(better, vj, b2)
    w1 = sin_[0]
    w2 = sin_[0]
    for j in range(1, epg):
        w1 = jnp.where(l1 == j, sin_[j], w1)
        w2 = jnp.where(l2 == j, sin_[j], w2)
    tot = w1 + w2
    w1 = w1 / tot
    w2 = w2 / tot
    grp_ref[...] = gidx
    zero = jnp.zeros_like(w1)
    rid = lax.broadcasted_iota(I32, (LANES, tm), 0)
    info = jnp.zeros((LANES, tm), F32)
    for j in range(epg):
        cwj = jnp.where(l1 == j, w1, zero) + jnp.where(l2 == j, w2, zero)
        info = jnp.where(rid == j, cwj, info)
    info_ref[...] = info.T


def _router(x, g, sc, sh, rw_t, rb_col, tm=512):
    s, d = x.shape
    row = _full((1, d))
    return pl.pallas_call(
        _router_body,
        out_shape=(jax.ShapeDtypeStruct((1, s), I32), jax.ShapeDtypeStruct((s, LANES), F32)),
        grid=(s // tm,),
        in_specs=[
            pl.BlockSpec((tm, d), lambda i: (i, 0)),
            row, row, row,
            _full((N_EXPERTS, d)),
            _full((N_EXPERTS, 1)),
        ],
        out_specs=(pl.BlockSpec((1, tm), lambda i: (0, i)), pl.BlockSpec((tm, LANES), lambda i: (i, 0))),
        scratch_shapes=[pltpu.VMEM((tm, d), F32)],
        compiler_params=_cparams(("arbitrary",), 32),
        name="router",
    )(x, g, sc, sh, rw_t, rb_col)


DMA_WINDOW = 64


def _row_copy(src, dst, i_src, i_dst, sem):
    return pltpu.make_async_copy(src.at[pl.ds(i_src, 1), :], dst.at[pl.ds(i_dst, 1), :], sem)


ZERO_ROWS = 64


def _permute_body(dest_ref, zf_ref, x_hbm, r_hbm, xs_hbm, rs_hbm, zx, zr, sem, *, cm):
    s = x_hbm.shape[0]
    nc = xs_hbm.shape[0] // cm
    zx[...] = jnp.zeros(zx.shape, F32)
    zr[...] = jnp.zeros(zr.shape, F32)

    def zero_copies(c):
        out = []
        for q in range(cm // ZERO_ROWS):
            r0 = c * cm + q * ZERO_ROWS
            out.append(pltpu.make_async_copy(zx, xs_hbm.at[pl.ds(r0, ZERO_ROWS), :], sem.at[0]))
            out.append(pltpu.make_async_copy(zr, rs_hbm.at[pl.ds(r0, ZERO_ROWS), :], sem.at[1]))
        return out

    def zstart(c, carry):
        @pl.when(zf_ref[c] != 0)
        def _():
            for cp in zero_copies(c):
                cp.start()
        return carry

    def zwait(c, carry):
        @pl.when(zf_ref[c] != 0)
        def _():
            for cp in zero_copies(c):
                cp.wait()
        return carry

    lax.fori_loop(0, nc, zstart, 0)
    lax.fori_loop(0, nc, zwait, 0)

    def wait_pair():
        _row_copy(x_hbm, xs_hbm, 0, 0, sem.at[0]).wait()
        _row_copy(r_hbm, rs_hbm, 0, 0, sem.at[1]).wait()

    def issue(t, carry):
        @pl.when(t >= DMA_WINDOW)
        def _():
            wait_pair()

        p = dest_ref[t]
        _row_copy(x_hbm, xs_hbm, t, p, sem.at[0]).start()
        _row_copy(r_hbm, rs_hbm, t, p, sem.at[1]).start()
        return carry

    lax.fori_loop(0, s, issue, 0)

    def drain(t, carry):
        wait_pair()
        return carry

    lax.fori_loop(0, min(DMA_WINDOW, s), drain, 0)


def _permute(dest, zflag, x, rinfo, p_rows):
    s, d = x.shape
    anyspec = pl.BlockSpec(memory_space=pl.ANY)
    body = functools.partial(_permute_body, cm=MOE_CHUNK)
    return pl.pallas_call(
        body,
        out_shape=(jax.ShapeDtypeStruct((p_rows, d), F32), jax.ShapeDtypeStruct((p_rows, LANES), F32)),
        grid_spec=pltpu.PrefetchScalarGridSpec(
            num_scalar_prefetch=2,
            grid=(1,),
            in_specs=[anyspec, anyspec],
            out_specs=(anyspec, anyspec),
            scratch_shapes=[pltpu.VMEM((ZERO_ROWS, d), F32), pltpu.VMEM((ZERO_ROWS, LANES), F32),
                            pltpu.SemaphoreType.DMA((2,))],
        ),
        compiler_params=_cparams(("arbitrary",), 16),
        name="moe_permute",
    )(dest, zflag, x, rinfo)


def _unpermute_body(dest_ref, ys_hbm, o_hbm, sem):
    s = o_hbm.shape[0]

    def issue(t, carry):
        @pl.when(t >= DMA_WINDOW)
        def _():
            _row_copy(ys_hbm, o_hbm, 0, 0, sem.at[0]).wait()

        _row_copy(ys_hbm, o_hbm, dest_ref[t], t, sem.at[0]).start()
        return carry

    lax.fori_loop(0, s, issue, 0)

    def drain(t, carry):
        _row_copy(ys_hbm, o_hbm, 0, 0, sem.at[0]).wait()
        return carry

    lax.fori_loop(0, min(DMA_WINDOW, s), drain, 0)


def _unpermute(dest, ys, s):
    d = ys.shape[1]
    anyspec = pl.BlockSpec(memory_space=pl.ANY)
    return pl.pallas_call(
        _unpermute_body,
        out_shape=jax.ShapeDtypeStruct((s, d), F32),
        grid_spec=pltpu.PrefetchScalarGridSpec(
            num_scalar_prefetch=1,
            grid=(1,),
            in_specs=[anyspec],
            out_specs=anyspec,
            scratch_shapes=[pltpu.SemaphoreType.DMA((1,))],
        ),
        compiler_params=_cparams(("arbitrary",), 16),
        name="moe_unpermute",
    )(dest, ys)


def _moe_body(gid_ref, nu_ref, xs_ref, rs_ref, g_ref, sc_ref, sh_ref, g2_ref, wg_ref, wu_ref, wd_ref,
              o_ref, h_scr, acc_scr, *, nslot):
    c = pl.program_id(0)
    sl = pl.program_id(1)
    used = c < nu_ref[0]
    rows = h_scr.shape[0]

    @pl.when(jnp.logical_and(used, sl == 0))
    def _():
        _norm_mod_store(xs_ref, g_ref, sc_ref, sh_ref, h_scr, rows)
        acc_scr[...] = jnp.zeros(acc_scr.shape, F32)

    @pl.when(used)
    def _():
        h = h_scr[...]
        gate = jnp.dot(h, wg_ref[0, 0].astype(BF16), preferred_element_type=F32)
        up = jnp.dot(h, wu_ref[0, 0].astype(BF16), preferred_element_type=F32)
        j = sl // MOE_FF_SPLIT
        info = rs_ref[...]
        cw = jnp.zeros((rows, 1), F32)
        for e in range(EXPERTS_PER_GROUP):
            cw = jnp.where(j == e, info[:, e:e + 1], cw)
        act = (gate * jax.nn.sigmoid(gate)) * up * cw
        acc_scr[...] += jnp.dot(act.astype(BF16), wd_ref[0, 0].astype(BF16), preferred_element_type=F32)

    @pl.when(jnp.logical_and(used, sl == nslot - 1))
    def _():
        o_ref[...] = xs_ref[...] + g2_ref[...] * acc_scr[...]

    @pl.when(jnp.logical_and(jnp.logical_not(used), sl == 0))
    def _():
        o_ref[...] = jnp.zeros(o_ref.shape, F32)


def _moe_experts(gid, nused, xs, rs, g, sc, sh, g2, w_gate, w_up, w_down, layer):
    p_rows, d = xs.shape
    cm = MOE_CHUNK
    nc = p_rows // cm
    nslot = EXPERTS_PER_GROUP * MOE_FF_SPLIT
    fs = EXPERT_FF // MOE_FF_SPLIT

    def eidx(c, sl, gid_ref, nu_ref):
        last = nu_ref[0] - 1
        cc = jnp.minimum(c, last)
        used = c <= last
        e = EXPERTS_PER_GROUP * gid_ref[cc] + jnp.where(used, sl // MOE_FF_SPLIT, EXPERTS_PER_GROUP - 1)
        half = jnp.where(used, sl % MOE_FF_SPLIT, MOE_FF_SPLIT - 1)
        return cc, e, half

    def xmap(c, sl, gid_ref, nu_ref):
        return (jnp.minimum(c, nu_ref[0] - 1), 0)

    def wgmap(c, sl, gid_ref, nu_ref):
        _, e, half = eidx(c, sl, gid_ref, nu_ref)
        return (layer, e, 0, half)

    def wdmap(c, sl, gid_ref, nu_ref):
        _, e, half = eidx(c, sl, gid_ref, nu_ref)
        return (layer, e, half, 0)

    row = pl.BlockSpec((1, d), lambda c, sl, a, b: (0, 0))
    body = functools.partial(_moe_body, nslot=nslot)
    return pl.pallas_call(
        body,
        out_shape=jax.ShapeDtypeStruct((p_rows, d), F32),
        grid_spec=pltpu.PrefetchScalarGridSpec(
            num_scalar_prefetch=2,
            grid=(nc, nslot),
            in_specs=[
                pl.BlockSpec((cm, d), xmap),
                pl.BlockSpec((cm, LANES), xmap),
                row, row, row, row,
                pl.BlockSpec((1, 1, d, fs), wgmap),
                pl.BlockSpec((1, 1, d, fs), wgmap),
                pl.BlockSpec((1, 1, fs, d), wdmap),
            ],
            out_specs=pl.BlockSpec((cm, d), lambda c, sl, a, b: (c, 0)),
            scratch_shapes=[pltpu.VMEM((cm, d), BF16), pltpu.VMEM((cm, d), F32)],
        ),
        compiler_params=_cparams(("arbitrary", "arbitrary"), 56),
        name="moe_experts",
    )(gid, nused, xs, rs, g, sc, sh, g2, w_gate, w_up, w_down)


def _dispatch_tables(grp, nc):
    cm = MOE_CHUNK
    ng = N_EXPERT_GROUPS
    oh = (grp[:, None] == jnp.arange(ng, dtype=I32)[None, :]).astype(I32)
    cs = jnp.cumsum(oh, axis=0)
    counts = cs[-1]
    rank = jnp.sum(cs * oh, axis=1) - 1
    nch = (counts + cm - 1) // cm
    cum = jnp.cumsum(nch)
    row0 = (cum - nch) * cm
    dest = jnp.sum(oh * row0[None, :], axis=1) + rank
    nused = cum[-1:]
    gid = jnp.sum((jnp.arange(nc, dtype=I32)[:, None] >= cum[None, :]).astype(I32), axis=1)
    gid = jnp.minimum(gid, ng - 1)
    cidx = jnp.arange(nc, dtype=I32)
    partial = jnp.any((cidx[:, None] == (cum - 1)[None, :]) & (nch > 0)[None, :], axis=1)
    zflag = jnp.logical_or(partial, cidx >= cum[-1])
    return dest.astype(I32), zflag.astype(I32), gid.astype(I32), nused.astype(I32)


def _moe_layer(x, g, sc, sh, g2, rw_t, rb_col, w_gate, w_up, w_down, layer):
    s, d = x.shape
    nc = s // MOE_CHUNK + N_EXPERT_GROUPS
    grp, rinfo = _router(x, g, sc, sh, rw_t, rb_col)
    dest, zflag, gid, nused = _dispatch_tables(grp[0], nc)
    xs, rs = _permute(dest, zflag, x, rinfo, nc * MOE_CHUNK)
    ys = _moe_experts(gid, nused, xs, rs, g, sc, sh, g2, w_gate, w_up, w_down, layer)
    return _unpermute(dest, ys, s)


def _final_body(x_ref, g_ref, o_ref):
    x = x_ref[...]
    ms = jnp.mean(x * x, axis=-1, keepdims=True)
    o_ref[...] = (x * lax.rsqrt(ms + EPS)) * g_ref[...]


def _final_norm(x, g, tm=256):
    s, d = x.shape
    return pl.pallas_call(
        _final_body,
        out_shape=jax.ShapeDtypeStruct((s, d), F32),
        grid=(s // tm,),
        in_specs=[pl.BlockSpec((tm, d), lambda i: (i, 0)), _full((1, d))],
        out_specs=pl.BlockSpec((tm, d), lambda i: (i, 0)),
        compiler_params=_cparams(("arbitrary",), 32),
        name="final_norm",
    )(x, g)


def _swap_halves(w):
    half = w.shape[-1] // 2
    return jnp.concatenate([w[..., half:], w[..., :half]], axis=-1)


def kernel(x, c, positions, ada_w, ada_b, norm_g, final_g, conv_w1, conv_b1, conv_wdw, conv_bdw, conv_ln_g, conv_ln_b, conv_w2, conv_b2, mla_w_down, mla_g_q, mla_g_kv, mla_w_uq, mla_w_ukv, mla_w_o, fnet_w, fnet_b, pool_w, pool_scale, router_w, router_b, moe_w_gate, moe_w_up, moe_w_down):
    b, s, d = x.shape
    assert b == 1 and d == D_MODEL
    xs = x.reshape(s, d)
    mod = _ada_mod(c, ada_w, ada_b).reshape(DEPTH, 6, 1, d)
    rw_t = router_w.T
    rb_col = router_b.reshape(N_EXPERTS, 1)
    zero_bias = jnp.zeros((1, d), F32)

    for i in range(DEPTH):
        sh1, sc1, g1, sh2, sc2, g2 = [mod[i, k] for k in range(6)]
        ng1 = norm_g[i, 0].reshape(1, d)
        ng2 = norm_g[i, 1].reshape(1, d)
        kind, j = i % 4, i // 4
        if kind == 0:
            u = _conv_glu(xs, ng1, sc1, sh1, conv_w1[j], conv_b1[j])
            dd = _dwconv_ln_silu(u, conv_wdw[j].reshape(CONV_WIDTH, d), conv_bdw[j].reshape(1, d),
                                 conv_ln_g[j].reshape(1, d), conv_ln_b[j].reshape(1, d))
            xs = _mm_res(dd, conv_w2[j], conv_b2[j].reshape(1, d), xs, g1)
        elif kind == 1:
            wd = mla_w_down[j]
            w_ext = jnp.concatenate([wd, _swap_halves(wd[:, Q_LORA + KV_LORA:])], axis=1)
            inv = ROPE_THETA ** (-jnp.arange(0, QK_ROPE_DIM, 2, dtype=F32) / QK_ROPE_DIM)
            zeros64 = jnp.zeros((QK_ROPE_DIM,), F32)
            half = QK_ROPE_DIM // 2
            rope_c = jnp.zeros((8, LANES), F32)
            rope_c = rope_c.at[0].set(jnp.concatenate([inv, inv, zeros64]))
            rope_c = rope_c.at[1].set(jnp.concatenate([jnp.ones((QK_ROPE_DIM,), F32), zeros64]))
            rope_c = rope_c.at[2].set(jnp.concatenate([-jnp.ones((half,), F32), jnp.ones((half,), F32), zeros64]))
            cq, ckv, kpe, cz, sz = _mla_down(xs, ng1, sc1, sh1, w_ext, mla_g_q[j].reshape(1, Q_LORA),
                                             mla_g_kv[j].reshape(1, KV_LORA), positions.reshape(s, 1), rope_c)
            wq = mla_w_uq[j]
            wq_ext = jnp.concatenate([wq, _swap_halves(wq[..., QK_NOPE_DIM:])], axis=-1)
            wq_ext = jnp.transpose(wq_ext, (1, 0, 2))
            wkv2d = mla_w_ukv[j].reshape(KV_LORA, N_HEADS * (QK_NOPE_DIM + V_HEAD_DIM))
            qh, kh, vh = _mla_up(cq, ckv, kpe, cz, sz, wq_ext, wkv2d)
            o = _attention(qh, kh, vh)
            xs = _mm_res(o, mla_w_o[j].reshape(N_HEADS * V_HEAD_DIM, d), zero_bias, xs, g1)
        elif kind == 2:
            cs, m1, ct, st, gmat = _dft_tables(s)
            a4 = _fnet_stage1(xs, ng1, sc1, sh1, cs, m1, ct, st)
            f = _fnet_stage2(a4, gmat)
            xs = _mm_res(f.reshape(s, d), fnet_w[j], fnet_b[j].reshape(1, d), xs, g1)
        else:
            xs = _pool_layer(xs, ng1, sc1, sh1, pool_w[j], pool_scale[j].reshape(1, d), g1)
        xs = _moe_layer(xs, ng2, sc2, sh2, g2, rw_t, rb_col, moe_w_gate, moe_w_up, moe_w_down, i)

    return _final_norm(xs, final_g.reshape(1, d)).reshape(b, s, d)
```

```python
import functools
import math

import numpy as np
import jax
import jax.numpy as jnp
from jax import lax
from jax.experimental import pallas as pl
from jax.experimental.pallas import tpu as pltpu

F32 = jnp.float32
BF16 = jnp.bfloat16
I32 = jnp.int32

D_MODEL = 2048
DEPTH = 4
EPS = 1e-6
CONV_WIDTH = 31
CONV_PAD = CONV_WIDTH // 2
N_HEADS = 16
Q_LORA = 512
KV_LORA = 512
QK_NOPE_DIM = 128
QK_ROPE_DIM = 64
V_HEAD_DIM = 128
QK_HEAD_DIM = QK_NOPE_DIM + QK_ROPE_DIM
ROPE_THETA = 10000.0
FNET_GROUPS = 4
FNET_GROUP_DIM = D_MODEL // FNET_GROUPS
POOL_WINDOWS = (2, 4, 8, 16)
POOL_GROUP_DIM = D_MODEL // len(POOL_WINDOWS)
N_EXPERTS = 16
N_EXPERT_GROUPS = 4
EXPERTS_PER_GROUP = 4
EXPERT_FF = 512

LANES = 128
HEAD_PAD = 256
DFT_N1 = 128
MOE_CHUNK = 512
MOE_FF_SPLIT = 2
MIB = 1 << 20


def _cparams(sem, vmem_mib):
    return pltpu.CompilerParams(dimension_semantics=sem, vmem_limit_bytes=vmem_mib * MIB)


def _full(shape):
    nd = len(shape)
    return pl.BlockSpec(shape, lambda *_: (0,) * nd)


def _norm_mod_rows(x, g, sp, sh):
    ms = jnp.mean(x * x, axis=-1, keepdims=True)
    y = x * lax.rsqrt(ms + EPS)
    return (y * g) * sp + sh


def _norm_mod_store(x_ref, g_ref, sc_ref, sh_ref, out_ref, rows, chunk=32):
    g = g_ref[...]
    sp = 1.0 + sc_ref[...]
    sh = sh_ref[...]

    def body(c, carry):
        r = pl.multiple_of(c * chunk, chunk)
        x = x_ref[pl.ds(r, chunk), :]
        out_ref[pl.ds(r, chunk), :] = _norm_mod_rows(x, g, sp, sh).astype(out_ref.dtype)
        return carry

    lax.fori_loop(0, rows // chunk, body, 0)


def _mod_body(c_ref, w_ref, b_ref, o_ref):
    k_dim = c_ref.shape[0]
    tn = o_ref.shape[-1]
    rc = 16

    def body(k, acc):
        r = pl.multiple_of(k * rc, rc)
        c = c_ref[pl.ds(r, rc), :]
        ca = c * jax.nn.sigmoid(c)
        return acc + w_ref[0, pl.ds(r, rc), :] * ca

    acc = lax.fori_loop(0, k_dim // rc, body, jnp.zeros((rc, tn), F32), unroll=4)
    o_ref[0] = jnp.sum(acc, axis=0, keepdims=True) + b_ref[0]


def _ada_mod(c, ada_w, ada_b):
    depth, d, n = ada_w.shape
    tn = 1024
    return pl.pallas_call(
        _mod_body,
        out_shape=jax.ShapeDtypeStruct((depth, 1, n), F32),
        grid=(depth, n // tn),
        in_specs=[
            _full((d, 1)),
            pl.BlockSpec((1, d, tn), lambda i, j: (i, 0, j)),
            pl.BlockSpec((1, 1, tn), lambda i, j: (i, 0, j)),
        ],
        out_specs=pl.BlockSpec((1, 1, tn), lambda i, j: (i, 0, j)),
        compiler_params=_cparams(("arbitrary", "arbitrary"), 40),
        name="ada_mod",
    )(c.reshape(d, 1), ada_w, ada_b.reshape(depth, 1, n))


def _mm_res_body(a_ref, w_ref, b_ref, x_ref, g_ref, o_ref):
    y = jnp.dot(a_ref[...], w_ref[...].astype(BF16), preferred_element_type=F32) + b_ref[...]
    o_ref[...] = x_ref[...] + g_ref[...] * y


def _mm_res(a, w, b, x, gate, tm=1024, tn=512):
    m, k = a.shape
    n = w.shape[1]
    return pl.pallas_call(
        _mm_res_body,
        out_shape=jax.ShapeDtypeStruct((m, n), F32),
        grid=(m // tm, n // tn),
        in_specs=[
            pl.BlockSpec((tm, k), lambda i, j: (i, 0)),
            pl.BlockSpec((k, tn), lambda i, j: (0, j)),
            pl.BlockSpec((1, tn), lambda i, j: (0, j)),
            pl.BlockSpec((tm, tn), lambda i, j: (i, j)),
            pl.BlockSpec((1, tn), lambda i, j: (0, j)),
        ],
        out_specs=pl.BlockSpec((tm, tn), lambda i, j: (i, j)),
        compiler_params=_cparams(("arbitrary", "arbitrary"), 48),
        name="mm_res",
    )(a, w, b, x, gate)


def _conv1_body(x_ref, g_ref, sc_ref, sh_ref, wa_ref, wb_ref, ba_ref, bb_ref, u_ref, h_scr):
    @pl.when(pl.program_id(1) == 0)
    def _():
        _norm_mod_store(x_ref, g_ref, sc_ref, sh_ref, h_scr, h_scr.shape[0])

    h = h_scr[...]
    a = jnp.dot(h, wa_ref[...].astype(BF16), preferred_element_type=F32) + ba_ref[...]
    b = jnp.dot(h, wb_ref[...].astype(BF16), preferred_element_type=F32) + bb_ref[...]
    u_ref[...] = a * jax.nn.sigmoid(b)


def _conv_glu(x, g, sc, sh, w1, b1, tm=1024, tn=512):
    s, d = x.shape
    nb = d // tn
    b1r = b1.reshape(1, 2 * d)
    row = _full((1, d))
    return pl.pallas_call(
        _conv1_body,
        out_shape=jax.ShapeDtypeStruct((s, d), F32),
        grid=(s // tm, nb),
        in_specs=[
            pl.BlockSpec((tm, d), lambda i, j: (i, 0)),
            row, row, row,
            pl.BlockSpec((d, tn), lambda i, j: (0, j)),
            pl.BlockSpec((d, tn), lambda i, j: (0, j + nb)),
            pl.BlockSpec((1, tn), lambda i, j: (0, j)),
            pl.BlockSpec((1, tn), lambda i, j: (0, j + nb)),
        ],
        out_specs=pl.BlockSpec((tm, tn), lambda i, j: (i, j)),
        scratch_shapes=[pltpu.VMEM((tm, d), BF16)],
        compiler_params=_cparams(("arbitrary", "arbitrary"), 56),
        name="conv_glu",
    )(x, g, sc, sh, w1, w1, b1r, b1r)


def _dwconv_body(up_ref, u_ref, un_ref, w_ref, bdw_ref, lg_ref, lb_ref, d_ref, buf, cv, *, ts, cw, ncw):
    i = pl.program_id(0)
    cj = pl.program_id(1)
    ni = pl.num_programs(0)
    halo = 16
    buf[0:halo, :] = jnp.where(i > 0, up_ref[...], 0.0)
    buf[halo:halo + ts, :] = u_ref[...]
    buf[halo + ts:2 * halo + ts, :] = jnp.where(i < ni - 1, un_ref[...], 0.0)
    rc = 32
    for lc in range(cw // LANES):
        ls = slice(lc * LANES, (lc + 1) * LANES)
        wcol = w_ref[:, ls]
        bcol = bdw_ref[:, ls]

        def body(r, carry, ls=ls, wcol=wcol, bcol=bcol):
            r0 = pl.multiple_of(r * rc, rc)
            wrows = rc + 2 * halo
            win = buf[pl.ds(r0, wrows), ls]
            acc = jnp.zeros((rc, LANES), F32)
            for phase in range(8):
                shifted = win if phase == 0 else pltpu.roll(win, wrows - phase, 0)
                for k in range(CONV_WIDTH):
                    off = halo - CONV_PAD + k
                    if off % 8 == phase:
                        acc = acc + shifted[off - phase:off - phase + rc, :] * wcol[k:k + 1, :]
            cv[cj, pl.ds(r0, rc), ls] = acc + bcol
            return carry

        lax.fori_loop(0, ts // rc, body, 0)

    @pl.when(cj == ncw - 1)
    def _():
        d_model = ncw * cw
        rc2 = 32

        def body2(r, carry):
            r0 = pl.multiple_of(r * rc2, rc2)
            parts = [cv[c, pl.ds(r0, rc2), :] for c in range(ncw)]
            tot = parts[0].sum(axis=-1, keepdims=True)
            for p in parts[1:]:
                tot = tot + p.sum(axis=-1, keepdims=True)
            mu = tot / d_model
            cen = [p - mu for p in parts]
            sq = (cen[0] * cen[0]).sum(axis=-1, keepdims=True)
            for p in cen[1:]:
                sq = sq + (p * p).sum(axis=-1, keepdims=True)
            rinv = lax.rsqrt(sq / d_model + EPS)
            for c in range(ncw):
                cs = slice(c * cw, (c + 1) * cw)
                y = (cen[c] * rinv) * lg_ref[:, cs] + lb_ref[:, cs]
                d_ref[pl.ds(r0, rc2), cs] = (y * jax.nn.sigmoid(y)).astype(d_ref.dtype)
            return carry

        lax.fori_loop(0, ts // rc2, body2, 0)


def _dwconv_ln_silu(u, wdw, bdw, ln_g, ln_b, ts=512, cw=512):
    s, d = u.shape
    ncw = d // cw
    hb = ts // 16
    nhb = s // 16
    body = functools.partial(_dwconv_body, ts=ts, cw=cw, ncw=ncw)
    return pl.pallas_call(
        body,
        out_shape=jax.ShapeDtypeStruct((s, d), BF16),
        grid=(s // ts, ncw),
        in_specs=[
            pl.BlockSpec((16, cw), lambda i, j: (jnp.maximum(i * hb - 1, 0), j)),
            pl.BlockSpec((ts, cw), lambda i, j: (i, j)),
            pl.BlockSpec((16, cw), lambda i, j: (jnp.minimum((i + 1) * hb, nhb - 1), j)),
            pl.BlockSpec((CONV_WIDTH, cw), lambda i, j: (0, j)),
            pl.BlockSpec((1, cw), lambda i, j: (0, j)),
            _full((1, d)),
            _full((1, d)),
        ],
        out_specs=pl.BlockSpec((ts, d), lambda i, j: (i, 0)),
        scratch_shapes=[pltpu.VMEM((ts + 32, cw), F32), pltpu.VMEM((ncw, ts, cw), F32)],
        compiler_params=_cparams(("arbitrary", "arbitrary"), 32),
        name="dwconv_ln_silu",
    )(u, u, u, wdw, bdw, ln_g, ln_b)


def _rope_rot(t, cz, sz):
    return t * cz + pltpu.roll(t, QK_ROPE_DIM, 1) * sz


def _mla_down_body(x_ref, g_ref, sc_ref, sh_ref, w_ref, gq_ref, gkv_ref, pos_ref, rc_ref,
                   cq_ref, ckv_ref, kpe_ref, cz_ref, sz_ref, h_scr, w_scr):
    @pl.when(pl.program_id(0) == 0)
    def _():
        w_scr[...] = w_ref[...].astype(BF16)

    _norm_mod_store(x_ref, g_ref, sc_ref, sh_ref, h_scr, h_scr.shape[0])
    down = jnp.dot(h_scr[...], w_scr[...], preferred_element_type=F32)
    cq = down[:, :Q_LORA]
    ckv = down[:, Q_LORA:Q_LORA + KV_LORA]
    cq_ref[...] = (cq * lax.rsqrt(jnp.mean(cq * cq, axis=-1, keepdims=True) + EPS) * gq_ref[...]).astype(BF16)
    ckv_ref[...] = (ckv * lax.rsqrt(jnp.mean(ckv * ckv, axis=-1, keepdims=True) + EPS) * gkv_ref[...]).astype(BF16)
    ang = pos_ref[...].astype(F32) * rc_ref[0:1, :]
    cz = jnp.cos(ang) * rc_ref[1:2, :]
    sz = jnp.sin(ang) * rc_ref[2:3, :]
    cz_ref[...] = cz
    sz_ref[...] = sz
    kpe_ref[...] = _rope_rot(down[:, Q_LORA + KV_LORA:], cz, sz).astype(BF16)


def _mla_down(x, g, sc, sh, w_ext, gq, gkv, pos_col, rope_c, tm=512):
    s, d = x.shape
    n = w_ext.shape[1]
    row = _full((1, d))
    return pl.pallas_call(
        _mla_down_body,
        out_shape=(
            jax.ShapeDtypeStruct((s, Q_LORA), BF16),
            jax.ShapeDtypeStruct((s, KV_LORA), BF16),
            jax.ShapeDtypeStruct((s, LANES), BF16),
            jax.ShapeDtypeStruct((s, LANES), F32),
            jax.ShapeDtypeStruct((s, LANES), F32),
        ),
        grid=(s // tm,),
        in_specs=[
            pl.BlockSpec((tm, d), lambda i: (i, 0)),
            row, row, row,
            _full((d, n)),
            _full((1, Q_LORA)),
            _full((1, KV_LORA)),
            pl.BlockSpec((tm, 1), lambda i: (i, 0)),
            _full((8, LANES)),
        ],
        out_specs=(
            pl.BlockSpec((tm, Q_LORA), lambda i: (i, 0)),
            pl.BlockSpec((tm, KV_LORA), lambda i: (i, 0)),
            pl.BlockSpec((tm, LANES), lambda i: (i, 0)),
            pl.BlockSpec((tm, LANES), lambda i: (i, 0)),
            pl.BlockSpec((tm, LANES), lambda i: (i, 0)),
        ),
        scratch_shapes=[pltpu.VMEM((tm, d), BF16), pltpu.VMEM((d, n), BF16)],
        compiler_params=_cparams(("arbitrary",), 56),
        name="mla_down",
    )(x, g, sc, sh, w_ext, gq, gkv, pos_col, rope_c)


def _mla_up_body(cq_ref, ckv_ref, kpe_ref, cz_ref, sz_ref, wq_ref, wkv_ref, q_ref, k_ref, v_ref, *, qscale):
    qf = jnp.dot(cq_ref[...], wq_ref[0].astype(BF16), preferred_element_type=F32)
    qpe = _rope_rot(qf[:, QK_NOPE_DIM:], cz_ref[...], sz_ref[...])
    q_ref[0] = (jnp.concatenate([qf[:, :QK_NOPE_DIM], qpe], axis=1) * qscale).astype(BF16)
    kv = jnp.dot(ckv_ref[...], wkv_ref[...].astype(BF16), preferred_element_type=F32)
    k_ref[0] = jnp.concatenate([kv[:, :QK_NOPE_DIM].astype(BF16), kpe_ref[...]], axis=1)
    v = kv[:, QK_NOPE_DIM:]
    v_ref[0] = jnp.concatenate([v, jnp.ones_like(v)], axis=1).astype(BF16)


def _mla_up(cq, ckv, kpe, cz, sz, wq_ext, wkv2d, tm=1024):
    s = cq.shape[0]
    qscale = (QK_HEAD_DIM ** -0.5) * math.log2(math.e)
    body = functools.partial(_mla_up_body, qscale=qscale)
    hs = jax.ShapeDtypeStruct((N_HEADS, s, HEAD_PAD), BF16)
    hspec = pl.BlockSpec((1, tm, HEAD_PAD), lambda i, h: (h, i, 0))
    return pl.pallas_call(
        body,
        out_shape=(hs, hs, hs),
        grid=(s // tm, N_HEADS),
        in_specs=[
            pl.BlockSpec((tm, Q_LORA), lambda i, h: (i, 0)),
            pl.BlockSpec((tm, KV_LORA), lambda i, h: (i, 0)),
            pl.BlockSpec((tm, LANES), lambda i, h: (i, 0)),
            pl.BlockSpec((tm, LANES), lambda i, h: (i, 0)),
            pl.BlockSpec((tm, LANES), lambda i, h: (i, 0)),
            pl.BlockSpec((1, Q_LORA, HEAD_PAD), lambda i, h: (h, 0, 0)),
            pl.BlockSpec((KV_LORA, HEAD_PAD), lambda i, h: (0, h)),
        ],
        out_specs=(hspec, hspec, hspec),
        compiler_params=_cparams(("arbitrary", "arbitrary"), 32),
        name="mla_up",
    )(cq, ckv, kpe, cz, sz, wq_ext, wkv2d)


ATTN_SUB_ROWS = 256


def _attn_body(q_ref, k_ref, v_ref, o_ref, m_scr, acc_scr, *, tk, nk):
    tq = q_ref.shape[1]
    m_scr[...] = jnp.full(m_scr.shape, -jnp.inf, F32)
    acc_scr[...] = jnp.zeros(acc_scr.shape, F32)

    def body(c, carry):
        r = pl.multiple_of(c * tk, tk)
        k = k_ref[0, pl.ds(r, tk), :]
        v = v_ref[0, pl.ds(r, tk), :]
        for r0 in range(0, tq, ATTN_SUB_ROWS):
            rows = slice(r0, r0 + ATTN_SUB_ROWS)
            s = lax.dot_general(q_ref[0, rows, :], k, (((1,), (1,)), ((), ())), preferred_element_type=F32)
            m_old = m_scr[rows, :]
            m_new = jnp.maximum(m_old, jnp.max(s, axis=-1, keepdims=True))
            alpha = jnp.exp2(m_old - m_new)
            p = jnp.exp2(s - m_new).astype(BF16)
            acc_scr[rows, :] = acc_scr[rows, :] * alpha + jnp.dot(p, v, preferred_element_type=F32)
            m_scr[rows, :] = m_new
        return carry

    lax.fori_loop(0, nk, body, 0, unroll=2 if nk % 2 == 0 else 1)
    acc = acc_scr[...]
    o_ref[...] = (acc[:, :V_HEAD_DIM] / acc[:, V_HEAD_DIM:V_HEAD_DIM + 1]).astype(o_ref.dtype)


def _attention(q, k, v, tq=1024, tk=1024):
    nh, s, _ = q.shape
    tk = min(tk, s)
    tq = min(tq, s)
    body = functools.partial(_attn_body, tk=tk, nk=s // tk)
    return pl.pallas_call(
        body,
        out_shape=jax.ShapeDtypeStruct((s, nh * V_HEAD_DIM), BF16),
        grid=(nh, s // tq),
        in_specs=[
            pl.BlockSpec((1, tq, HEAD_PAD), lambda h, i: (h, i, 0)),
            pl.BlockSpec((1, s, HEAD_PAD), lambda h, i: (h, 0, 0)),
            pl.BlockSpec((1, s, HEAD_PAD), lambda h, i: (h, 0, 0)),
        ],
        out_specs=pl.BlockSpec((tq, V_HEAD_DIM), lambda h, i: (i, h)),
        scratch_shapes=[pltpu.VMEM((tq, 1), F32), pltpu.VMEM((tq, HEAD_PAD), F32)],
        compiler_params=_cparams(("arbitrary", "arbitrary"), 40),
        name="attention",
    )(q, k, v)


def _dft_tables(s):
    n1 = DFT_N1
    n2 = s // n1
    m = FNET_GROUP_DIM
    c = np.arange(m, dtype=np.float64)
    ang = 2.0 * np.pi * np.outer(c, c) / m
    cs = np.concatenate([np.cos(ang), np.sin(ang)], axis=1)
    a1 = 2.0 * np.pi * np.outer(np.arange(n1), np.arange(n1)) / n1
    c1, s1 = np.cos(a1), np.sin(a1)
    m1 = np.block([[c1, -s1], [-s1, -c1]])
    at = 2.0 * np.pi * np.outer(np.arange(n2), np.arange(n1)) / s
    ct = np.cos(at)[:, :, None]
    st = np.sin(at)[:, :, None]
    a2 = 2.0 * np.pi * np.outer(np.arange(n2), np.arange(n2)) / n2
    scale = 1.0 / math.sqrt(float(s) * m)
    jb = 16
    gmat = np.zeros((n2, jb, 2, n2, jb), dtype=np.float64)
    for j in range(jb):
        gmat[:, j, 0, :, j] = np.cos(a2) * scale
        gmat[:, j, 1, :, j] = np.sin(a2) * scale
    gmat = gmat.reshape(n2 * jb, 2 * n2 * jb)
    return (jnp.asarray(cs, BF16), jnp.asarray(m1, BF16), jnp.asarray(ct, F32), jnp.asarray(st, F32),
            jnp.asarray(gmat, BF16))


def _fnet1_body(x_ref, g_ref, sc_ref, sh_ref, cs_ref, m1_ref, ct_ref, st_ref, o_ref, pq_scr, *, nb):
    n1 = DFT_N1
    g = g_ref[...]
    sp = 1.0 + sc_ref[...]
    sh = sh_ref[...]
    cs = cs_ref[...]
    m1 = m1_ref[...]
    gd = FNET_GROUP_DIM
    for j in range(nb):
        h = _norm_mod_rows(x_ref[:, j, :], g, sp, sh).astype(BF16)
        for gi in range(FNET_GROUPS):
            cols = slice(gi * gd, (gi + 1) * gd)
            pq = jnp.dot(h[:, cols], cs, preferred_element_type=F32)
            pq_scr[0:n1, cols] = pq[:, :gd].astype(BF16)
            pq_scr[n1:2 * n1, cols] = pq[:, gd:].astype(BF16)
        a = jnp.dot(m1, pq_scr[...], preferred_element_type=F32)
        ar = a[:n1]
        ai = a[n1:]
        ct = ct_ref[j]
        st = st_ref[j]
        o_ref[0, j] = (ar * ct + ai * st).astype(BF16)
        o_ref[1, j] = (ai * ct - ar * st).astype(BF16)


def _fnet_stage1(x, g, sc, sh, cs, m1, ct, st, nb=8):
    s, d = x.shape
    n1 = DFT_N1
    n2 = s // n1
    body = functools.partial(_fnet1_body, nb=nb)
    row = _full((1, d))
    return pl.pallas_call(
        body,
        out_shape=jax.ShapeDtypeStruct((2, n2, n1, d), BF16),
        grid=(n2 // nb,),
        in_specs=[
            pl.BlockSpec((n1, nb, d), lambda b: (0, b, 0)),
            row, row, row,
            _full(cs.shape),
            _full(m1.shape),
            pl.BlockSpec((nb, n1, 1), lambda b: (b, 0, 0)),
            pl.BlockSpec((nb, n1, 1), lambda b: (b, 0, 0)),
        ],
        out_specs=pl.BlockSpec((2, nb, n1, d), lambda b: (0, b, 0, 0)),
        scratch_shapes=[pltpu.VMEM((2 * n1, d), BF16)],
        compiler_params=_cparams(("arbitrary",), 48),
        name="fnet_stage1",
    )(x.reshape(n1, n2, d), g, sc, sh, cs, m1, ct, st)


def _fnet2_body(a_ref, g_ref, o_ref):
    blk = a_ref[...]
    rows = blk.shape[0] * blk.shape[1] * blk.shape[2]
    b2 = blk.reshape(rows, blk.shape[3])
    f = jnp.dot(g_ref[...], b2, preferred_element_type=F32)
    o_ref[...] = f.astype(o_ref.dtype).reshape(o_ref.shape)


def _fnet_stage2(a4, gmat, jb=16):
    _, n2, n1, d = a4.shape
    return pl.pallas_call(
        _fnet2_body,
        out_shape=jax.ShapeDtypeStruct((n2, n1, d), BF16),
        grid=(n1 // jb,),
        in_specs=[
            pl.BlockSpec((2, n2, jb, d), lambda b: (0, 0, b, 0)),
            _full(gmat.shape),
        ],
        out_specs=pl.BlockSpec((n2, jb, d), lambda b: (0, b, 0)),
        compiler_params=_cparams(("arbitrary",), 48),
        name="fnet_stage2",
    )(a4, gmat)


def _pool_body(xp_ref, x_ref, xn_ref, g_ref, sc_ref, sh_ref, w_ref, ps_ref, g1_ref, o_ref, hbuf, w_scr, *, ts, seq):
    i = pl.program_id(0)
    ni = pl.num_programs(0)
    halo = 8

    @pl.when(i == 0)
    def _():
        w_scr[...] = w_ref[...].astype(BF16)

    g = g_ref[...]
    sp = 1.0 + sc_ref[...]
    sh = sh_ref[...]
    hbuf[0:halo, :] = jnp.where(i > 0, _norm_mod_rows(xp_ref[...], g, sp, sh), 0.0)
    _norm_mod_store(x_ref, g_ref, sc_ref, sh_ref, hbuf.at[pl.ds(halo, ts), :], ts)
    hbuf[halo + ts:2 * halo + ts, :] = jnp.where(i < ni - 1, _norm_mod_rows(xn_ref[...], g, sp, sh), 0.0)

    t = i * ts + lax.broadcasted_iota(I32, (ts, 1), 0)
    gd = POOL_GROUP_DIM
    for gi, win in enumerate(POOL_WINDOWS):
        cols = slice(gi * gd, (gi + 1) * gd)
        half = win // 2
        wsum = hbuf[halo - half:halo - half + ts, cols]
        for off in range(-half + 1, win - half):
            wsum = wsum + hbuf[halo + off:halo + off + ts, cols]
        count = (jnp.minimum(t + (win - half), seq) - jnp.maximum(t - half, 0)).astype(F32)
        mixed = wsum / count - hbuf[halo:halo + ts, cols]
        y = jnp.dot(mixed.astype(BF16), w_scr[gi], preferred_element_type=F32)
        o_ref[:, cols] = x_ref[:, cols] + g1_ref[:, cols] * (y * ps_ref[:, cols])


def _pool_layer(x, g, sc, sh, w_pool, pscale, g1, ts=512):
    s, d = x.shape
    hb = ts // 8
    nhb = s // 8
    body = functools.partial(_pool_body, ts=ts, seq=s)
    row = _full((1, d))
    return pl.pallas_call(
        body,
        out_shape=jax.ShapeDtypeStruct((s, d), F32),
        grid=(s // ts,),
        in_specs=[
            pl.BlockSpec((8, d), lambda i: (jnp.maximum(i * hb - 1, 0), 0)),
            pl.BlockSpec((ts, d), lambda i: (i, 0)),
            pl.BlockSpec((8, d), lambda i: (jnp.minimum((i + 1) * hb, nhb - 1), 0)),
            row, row, row,
            _full(w_pool.shape),
            row, row,
        ],
        out_specs=pl.BlockSpec((ts, d), lambda i: (i, 0)),
        scratch_shapes=[pltpu.VMEM((ts + 16, d), F32), pltpu.VMEM(w_pool.shape, BF16)],
        compiler_params=_cparams(("arbitrary",), 48),
        name="pool_layer",
    )(x, x, x, g, sc, sh, w_pool, pscale, g1)


SLAB_X = D_MODEL // LANES
SLAB_ROWS = SLAB_X + 8


def _router_body(x_ref, g_ref, sc_ref, sh_ref, rw_ref, rb_ref, grp_ref, slab_ref, h_scr):
    tm = h_scr.shape[0]
    _norm_mod_store(x_ref, g_ref, sc_ref, sh_ref, h_scr, tm)
    h = h_scr[...]
    h_hi = h.astype(BF16)
    h_lo = (h - h_hi.astype(F32)).astype(BF16)
    rw = rw_ref[...]
    rw_hi = rw.astype(BF16)
    rw_lo = (rw - rw_hi.astype(F32)).astype(BF16)
    nt = (((1,), (1,)), ((), ()))
    p_hi = lax.dot_general(jnp.concatenate([rw_hi, rw_lo], axis=0), h_hi, nt, preferred_element_type=F32)
    p_lo = lax.dot_general(rw_hi, h_lo, nt, preferred_element_type=F32)
    logits = p_hi[:N_EXPERTS] + (p_hi[N_EXPERTS:] + p_lo)
    sc = jax.nn.sigmoid(logits)
    sel = sc + rb_ref[...]
    epg = EXPERTS_PER_GROUP
    rows = [sel[e:e + 1, :] for e in range(N_EXPERTS)]
    srow = [sc[e:e + 1, :] for e in range(N_EXPERTS)]
    gscore = []
    for gi in range(N_EXPERT_GROUPS):
        r = rows[gi * epg:(gi + 1) * epg]
        best = None
        for a in range(epg):
            for b in range(a + 1, epg):
                ps = r[a] + r[b]
                best = ps if best is None else jnp.maximum(best, ps)
        gscore.append(best)
    gidx = jnp.zeros_like(gscore[0], dtype=I32)
    gbest = gscore[0]
    for gi in range(1, N_EXPERT_GROUPS):
        better = gscore[gi] > gbest
        gidx = jnp.where(better, gi, gidx)
        gbest = jnp.where(better, gscore[gi], gbest)
    vin = []
    sin_ = []
    for j in range(epg):
        v = rows[j]
        sv = srow[j]
        for gi in range(1, N_EXPERT_GROUPS):
            pick = gidx == gi
            v = jnp.where(pick, rows[gi * epg + j], v)
            sv = jnp.where(pick, srow[gi * epg + j], sv)
        vin.append(v)
        sin_.append(sv)
    l1 = jnp.zeros_like(gidx)
    b1 = vin[0]
    for j in range(1, epg):
        better = vin[j] > b1
        l1 = jnp.where(better, j, l1)
        b1 = jnp.where(better, vin[j], b1)
    neg = jnp.full_like(b1, -jnp.inf)
    l2 = jnp.zeros_like(gidx)
    b2 = neg
    for j in range(epg):
        vj = jnp.where(l1 == j, neg, vin[j])
        better = vj > b2
        l2 = jnp.where(better, j, l2)
        b2 = jnp.where(better, vj, b2)
    w1 = sin_[0]
    w2 = sin_[0]
    for j in range(1, epg):
        w1 = jnp.where(l1 == j, sin_[j], w1)
        w2 = jnp.where(l2 == j, sin_[j], w2)
    tot = w1 + w2
    w1 = w1 / tot
    w2 = w2 / tot
    grp_ref[...] = gidx
    zero = jnp.zeros_like(w1)
    rid = lax.broadcasted_iota(I32, (LANES, tm), 0)
    info = jnp.zeros((LANES, tm), F32)
    for j in range(epg):
        cwj = jnp.where(l1 == j, w1, zero) + jnp.where(l2 == j, w2, zero)
        info = jnp.where(rid == j, cwj, info)
    slab_ref[:, 0:SLAB_X, :] = x_ref[...].reshape(tm, SLAB_X, LANES)
    tail = jnp.concatenate([info.T, jnp.zeros((tm, (SLAB_ROWS - SLAB_X - 1) * LANES), F32)], axis=1)
    slab_ref[:, SLAB_X:SLAB_ROWS, :] = tail.reshape(tm, SLAB_ROWS - SLAB_X, LANES)


def _router(x, g, sc, sh, rw_t, rb_col, tm=512):
    s, d = x.shape
    row = _full((1, d))
    return pl.pallas_call(
        _router_body,
        out_shape=(jax.ShapeDtypeStruct((1, s), I32), jax.ShapeDtypeStruct((s, SLAB_ROWS, LANES), F32)),
        grid=(s // tm,),
        in_specs=[
            pl.BlockSpec((tm, d), lambda i: (i, 0)),
            row, row, row,
            _full((N_EXPERTS, d)),
            _full((N_EXPERTS, 1)),
        ],
        out_specs=(pl.BlockSpec((1, tm), lambda i: (0, i)),
                   pl.BlockSpec((tm, SLAB_ROWS, LANES), lambda i: (i, 0, 0))),
        scratch_shapes=[pltpu.VMEM((tm, d), F32)],
        compiler_params=_cparams(("arbitrary",), 40),
        name="router",
    )(x, g, sc, sh, rw_t, rb_col)


DMA_WINDOW = 64


def _slab_copy(src, dst, i_src, i_dst, sem):
    return pltpu.make_async_copy(src.at[pl.ds(i_src, 1)], dst.at[pl.ds(i_dst, 1)], sem)


ZERO_ROWS = 64


def _permute_body(dest_ref, zf_ref, x_hbm, xs_hbm, zx, sem, *, cm):
    s = x_hbm.shape[0]
    nc = xs_hbm.shape[0] // cm
    zx[...] = jnp.zeros(zx.shape, F32)

    def zero_copies(c):
        out = []
        for q in range(cm // ZERO_ROWS):
            r0 = c * cm + q * ZERO_ROWS
            out.append(pltpu.make_async_copy(zx, xs_hbm.at[pl.ds(r0, ZERO_ROWS)], sem.at[0]))
        return out

    def zstart(c, carry):
        @pl.when(zf_ref[c] != 0)
        def _():
            for cp in zero_copies(c):
                cp.start()
        return carry

    def zwait(c, carry):
        @pl.when(zf_ref[c] != 0)
        def _():
            for cp in zero_copies(c):
                cp.wait()
        return carry

    lax.fori_loop(0, nc, zstart, 0)
    lax.fori_loop(0, nc, zwait, 0)

    def issue(t, carry):
        @pl.when(t >= DMA_WINDOW)
        def _():
            _slab_copy(x_hbm, xs_hbm, 0, 0, sem.at[0]).wait()

        _slab_copy(x_hbm, xs_hbm, t, dest_ref[t], sem.at[0]).start()
        return carry

    lax.fori_loop(0, s, issue, 0)

    def drain(t, carry):
        _slab_copy(x_hbm, xs_hbm, 0, 0, sem.at[0]).wait()
        return carry

    lax.fori_loop(0, min(DMA_WINDOW, s), drain, 0)


def _permute(dest, zflag, slabs, p_rows):
    anyspec = pl.BlockSpec(memory_space=pl.ANY)
    body = functools.partial(_permute_body, cm=MOE_CHUNK)
    return pl.pallas_call(
        body,
        out_shape=jax.ShapeDtypeStruct((p_rows,) + slabs.shape[1:], F32),
        grid_spec=pltpu.PrefetchScalarGridSpec(
            num_scalar_prefetch=2,
            grid=(1,),
            in_specs=[anyspec],
            out_specs=anyspec,
            scratch_shapes=[pltpu.VMEM((ZERO_ROWS,) + slabs.shape[1:], F32), pltpu.SemaphoreType.DMA((1,))],
        ),
        compiler_params=_cparams(("arbitrary",), 16),
        name="moe_permute",
    )(dest, zflag, slabs)


def _unpermute_body(dest_ref, ys_hbm, o_hbm, sem):
    s = o_hbm.shape[0]

    def issue(t, carry):
        @pl.when(t >= DMA_WINDOW)
        def _():
            _slab_copy(ys_hbm, o_hbm, 0, 0, sem.at[0]).wait()

        _slab_copy(ys_hbm, o_hbm, dest_ref[t], t, sem.at[0]).start()
        return carry

    lax.fori_loop(0, s, issue, 0)

    def drain(t, carry):
        _slab_copy(ys_hbm, o_hbm, 0, 0, sem.at[0]).wait()
        return carry

    lax.fori_loop(0, min(DMA_WINDOW, s), drain, 0)


def _unpermute(dest, ys, s):
    anyspec = pl.BlockSpec(memory_space=pl.ANY)
    return pl.pallas_call(
        _unpermute_body,
        out_shape=jax.ShapeDtypeStruct((s,) + ys.shape[1:], F32),
        grid_spec=pltpu.PrefetchScalarGridSpec(
            num_scalar_prefetch=1,
            grid=(1,),
            in_specs=[anyspec],
            out_specs=anyspec,
            scratch_shapes=[pltpu.SemaphoreType.DMA((1,))],
        ),
        compiler_params=_cparams(("arbitrary",), 16),
        name="moe_unpermute",
    )(dest, ys)


def _slab_to_rows_body(x3_ref, o_ref):
    o_ref[...] = x3_ref[...].reshape(o_ref.shape)


def _slab_to_rows(x3, tm=512):
    s = x3.shape[0]
    d = SLAB_X * LANES
    return pl.pallas_call(
        _slab_to_rows_body,
        out_shape=jax.ShapeDtypeStruct((s, d), F32),
        grid=(s // tm,),
        in_specs=[pl.BlockSpec((tm, SLAB_X, LANES), lambda i: (i, 0, 0))],
        out_specs=pl.BlockSpec((tm, d), lambda i: (i, 0)),
        compiler_params=_cparams(("arbitrary",), 32),
        name="slab_to_rows",
    )(x3)


def _moe_body(gid_ref, nu_ref, xs_ref, g_ref, sc_ref, sh_ref, g2_ref, wg_ref, wu_ref, wd_ref,
              o_ref, h_scr, acc_scr, act_scr, *, nslot):
    c = pl.program_id(0)
    sl = pl.program_id(1)
    used = c < nu_ref[0]
    rows = h_scr.shape[0]

    @pl.when(jnp.logical_and(used, sl == 0))
    def _():
        g = g_ref[...]
        sp = 1.0 + sc_ref[...]
        sh = sh_ref[...]
        chunk = 32

        def body(k, carry):
            r = pl.multiple_of(k * chunk, chunk)
            x = xs_ref[pl.ds(r, chunk), 0:SLAB_X, :].reshape(chunk, SLAB_X * LANES)
            h_scr[pl.ds(r, chunk), :] = _norm_mod_rows(x, g, sp, sh).astype(BF16)
            return carry

        lax.fori_loop(0, rows // chunk, body, 0)
        acc_scr[...] = jnp.zeros(acc_scr.shape, F32)

    @pl.when(used)
    def _():
        h = h_scr[...]
        gate = jnp.dot(h, wg_ref[0, 0].astype(BF16), preferred_element_type=F32)
        up = jnp.dot(h, wu_ref[0, 0].astype(BF16), preferred_element_type=F32)
        j = sl // MOE_FF_SPLIT
        info = xs_ref[:, SLAB_X:SLAB_ROWS, :].reshape(rows, (SLAB_ROWS - SLAB_X) * LANES)[:, :LANES]
        cw = jnp.zeros((rows, 1), F32)
        for e in range(EXPERTS_PER_GROUP):
            cw = jnp.where(j == e, info[:, e:e + 1], cw)
        act = (gate * jax.nn.sigmoid(gate)) * up * cw
        part = sl % MOE_FF_SPLIT
        act_scr[part] = act.astype(BF16)

        @pl.when(part == MOE_FF_SPLIT - 1)
        def _():
            full = jnp.concatenate([act_scr[k] for k in range(MOE_FF_SPLIT)], axis=1)
            acc_scr[...] += jnp.dot(full, wd_ref[0, 0].astype(BF16), preferred_element_type=F32)

    @pl.when(jnp.logical_and(used, sl == nslot - 1))
    def _():
        y = g2_ref[...] * acc_scr[...]
        o_ref[...] = xs_ref[:, 0:SLAB_X, :] + y.reshape(rows, SLAB_X, LANES)

    @pl.when(jnp.logical_and(jnp.logical_not(used), sl == 0))
    def _():
        o_ref[...] = jnp.zeros(o_ref.shape, F32)


def _moe_experts(gid, nused, xs, g, sc, sh, g2, w_gate, w_up, w_down, layer):
    p_rows = xs.shape[0]
    d = SLAB_X * LANES
    cm = MOE_CHUNK
    nc = p_rows // cm
    nslot = EXPERTS_PER_GROUP * MOE_FF_SPLIT
    fs = EXPERT_FF // MOE_FF_SPLIT

    def eidx(c, sl, gid_ref, nu_ref):
        last = nu_ref[0] - 1
        cc = jnp.minimum(c, last)
        used = c <= last
        e = EXPERTS_PER_GROUP * gid_ref[cc] + jnp.where(used, sl // MOE_FF_SPLIT, EXPERTS_PER_GROUP - 1)
        half = jnp.where(used, sl % MOE_FF_SPLIT, MOE_FF_SPLIT - 1)
        return cc, e, half

    def xmap(c, sl, gid_ref, nu_ref):
        return (jnp.minimum(c, nu_ref[0] - 1), 0, 0)

    def wgmap(c, sl, gid_ref, nu_ref):
        _, e, half = eidx(c, sl, gid_ref, nu_ref)
        return (layer, e, 0, half)

    def wdmap(c, sl, gid_ref, nu_ref):
        _, e, _ = eidx(c, sl, gid_ref, nu_ref)
        return (layer, e, 0, 0)

    row = pl.BlockSpec((1, d), lambda c, sl, a, b: (0, 0))
    body = functools.partial(_moe_body, nslot=nslot)
    return pl.pallas_call(
        body,
        out_shape=jax.ShapeDtypeStruct((p_rows, SLAB_X, LANES), F32),
        grid_spec=pltpu.PrefetchScalarGridSpec(
            num_scalar_prefetch=2,
            grid=(nc, nslot),
            in_specs=[
                pl.BlockSpec((cm, SLAB_ROWS, LANES), xmap),
                row, row, row, row,
                pl.BlockSpec((1, 1, d, fs), wgmap),
                pl.BlockSpec((1, 1, d, fs), wgmap),
                pl.BlockSpec((1, 1, EXPERT_FF, d), wdmap),
            ],
            out_specs=pl.BlockSpec((cm, SLAB_X, LANES), lambda c, sl, a, b: (c, 0, 0)),
            scratch_shapes=[pltpu.VMEM((cm, d), BF16), pltpu.VMEM((cm, d), F32),
                            pltpu.VMEM((MOE_FF_SPLIT, cm, fs), BF16)],
        ),
        compiler_params=_cparams(("arbitrary", "arbitrary"), 56),
        name="moe_experts",
    )(gid, nused, xs, g, sc, sh, g2, w_gate, w_up, w_down)


def _dispatch_tables(grp, nc):
    cm = MOE_CHUNK
    ng = N_EXPERT_GROUPS
    oh = (grp[:, None] == jnp.arange(ng, dtype=I32)[None, :]).astype(I32)
    cs = jnp.cumsum(oh, axis=0)
    counts = cs[-1]
    rank = jnp.sum(cs * oh, axis=1) - 1
    nch = (counts + cm - 1) // cm
    cum = jnp.cumsum(nch)
    row0 = (cum - nch) * cm
    dest = jnp.sum(oh * row0[None, :], axis=1) + rank
    nused = cum[-1:]
    gid = jnp.sum((jnp.arange(nc, dtype=I32)[:, None] >= cum[None, :]).astype(I32), axis=1)
    gid = jnp.minimum(gid, ng - 1)
    cidx = jnp.arange(nc, dtype=I32)
    partial = jnp.any((cidx[:, None] == (cum - 1)[None, :]) & (nch > 0)[None, :], axis=1)
    zflag = jnp.logical_or(partial, cidx >= cum[-1])
    return dest.astype(I32), zflag.astype(I32), gid.astype(I32), nused.astype(I32)


def _moe_layer(x, g, sc, sh, g2, rw_t, rb_col, w_gate, w_up, w_down, layer):
    s, d = x.shape
    nc = s // MOE_CHUNK + N_EXPERT_GROUPS
    grp, slabs = _router(x, g, sc, sh, rw_t, rb_col)
    dest, zflag, gid, nused = _dispatch_tables(grp[0], nc)
    xs = _permute(dest, zflag, slabs, nc * MOE_CHUNK)
    ys = _moe_experts(gid, nused, xs, g, sc, sh, g2, w_gate, w_up, w_down, layer)
    return _slab_to_rows(_unpermute(dest, ys, s))


def _final_body(x_ref, g_ref, o_ref):
    x = x_ref[...]
    ms = jnp.mean(x * x, axis=-1, keepdims=True)
    o_ref[...] = (x * lax.rsqrt(ms + EPS)) * g_ref[...]


def _final_norm(x, g, tm=256):
    s, d = x.shape
    return pl.pallas_call(
        _final_body,
        out_shape=jax.ShapeDtypeStruct((s, d), F32),
        grid=(s // tm,),
        in_specs=[pl.BlockSpec((tm, d), lambda i: (i, 0)), _full((1, d))],
        out_specs=pl.BlockSpec((tm, d), lambda i: (i, 0)),
        compiler_params=_cparams(("arbitrary",), 32),
        name="final_norm",
    )(x, g)


def _swap_halves(w):
    half = w.shape[-1] // 2
    return jnp.concatenate([w[..., half:], w[..., :half]], axis=-1)


def kernel(x, c, positions, ada_w, ada_b, norm_g, final_g, conv_w1, conv_b1, conv_wdw, conv_bdw, conv_ln_g, conv_ln_b, conv_w2, conv_b2, mla_w_down, mla_g_q, mla_g_kv, mla_w_uq, mla_w_ukv, mla_w_o, fnet_w, fnet_b, pool_w, pool_scale, router_w, router_b, moe_w_gate, moe_w_up, moe_w_down):
    b, s, d = x.shape
    assert b == 1 and d == D_MODEL
    xs = x.reshape(s, d)
    mod = _ada_mod(c, ada_w, ada_b).reshape(DEPTH, 6, 1, d)
    rw_t = router_w.T
    rb_col = router_b.reshape(N_EXPERTS, 1)
    zero_bias = jnp.zeros((1, d), F32)

    for i in range(DEPTH):
        sh1, sc1, g1, sh2, sc2, g2 = [mod[i, k] for k in range(6)]
        ng1 = norm_g[i, 0].reshape(1, d)
        ng2 = norm_g[i, 1].reshape(1, d)
        kind, j = i % 4, i // 4
        if kind == 0:
            u = _conv_glu(xs, ng1, sc1, sh1, conv_w1[j], conv_b1[j])
            dd = _dwconv_ln_silu(u, conv_wdw[j].reshape(CONV_WIDTH, d), conv_bdw[j].reshape(1, d),
                                 conv_ln_g[j].reshape(1, d), conv_ln_b[j].reshape(1, d))
            xs = _mm_res(dd, conv_w2[j], conv_b2[j].reshape(1, d), xs, g1)
        elif kind == 1:
            wd = mla_w_down[j]
            w_ext = jnp.concatenate([wd, _swap_halves(wd[:, Q_LORA + KV_LORA:])], axis=1)
            inv = ROPE_THETA ** (-jnp.arange(0, QK_ROPE_DIM, 2, dtype=F32) / QK_ROPE_DIM)
            zeros64 = jnp.zeros((QK_ROPE_DIM,), F32)
            half = QK_ROPE_DIM // 2
            rope_c = jnp.zeros((8, LANES), F32)
            rope_c = rope_c.at[0].set(jnp.concatenate([inv, inv, zeros64]))
            rope_c = rope_c.at[1].set(jnp.concatenate([jnp.ones((QK_ROPE_DIM,), F32), zeros64]))
            rope_c = rope_c.at[2].set(jnp.concatenate([-jnp.ones((half,), F32), jnp.ones((half,), F32), zeros64]))
            cq, ckv, kpe, cz, sz = _mla_down(xs, ng1, sc1, sh1, w_ext, mla_g_q[j].reshape(1, Q_LORA),
                                             mla_g_kv[j].reshape(1, KV_LORA), positions.reshape(s, 1), rope_c)
            wq = mla_w_uq[j]
            wq_ext = jnp.concatenate([wq, _swap_halves(wq[..., QK_NOPE_DIM:])], axis=-1)
            wq_ext = jnp.transpose(wq_ext, (1, 0, 2))
            wkv2d = mla_w_ukv[j].reshape(KV_LORA, N_HEADS * (QK_NOPE_DIM + V_HEAD_DIM))
            qh, kh, vh = _mla_up(cq, ckv, kpe, cz, sz, wq_ext, wkv2d)
            o = _attention(qh, kh, vh)
            xs = _mm_res(o, mla_w_o[j].reshape(N_HEADS * V_HEAD_DIM, d), zero_bias, xs, g1)
        elif kind == 2:
            cs, m1, ct, st, gmat = _dft_tables(s)
            a4 = _fnet_stage1(xs, ng1, sc1, sh1, cs, m1, ct, st)
            f = _fnet_stage2(a4, gmat)
            xs = _mm_res(f.reshape(s, d), fnet_w[j], fnet_b[j].reshape(1, d), xs, g1)
        else:
            xs = _pool_layer(xs, ng1, sc1, sh1, pool_w[j], pool_scale[j].reshape(1, d), g1)
        xs = _moe_layer(xs, ng2, sc2, sh2, g2, rw_t, rb_col, moe_w_gate, moe_w_up, moe_w_down, i)

    return _final_norm(xs, final_g.reshape(1, d)).reshape(b, s, d)
```

```python
import functools
import math

import numpy as np
import jax
import jax.numpy as jnp
from jax import lax
from jax.experimental import pallas as pl
from jax.experimental.pallas import tpu as pltpu

F32 = jnp.float32
BF16 = jnp.bfloat16
I32 = jnp.int32

D_MODEL = 2048
DEPTH = 4
EPS = 1e-6
CONV_WIDTH = 31
CONV_PAD = CONV_WIDTH // 2
N_HEADS = 16
Q_LORA = 512
KV_LORA = 512
QK_NOPE_DIM = 128
QK_ROPE_DIM = 64
V_HEAD_DIM = 128
QK_HEAD_DIM = QK_NOPE_DIM + QK_ROPE_DIM
ROPE_THETA = 10000.0
FNET_GROUPS = 4
FNET_GROUP_DIM = D_MODEL // FNET_GROUPS
POOL_WINDOWS = (2, 4, 8, 16)
POOL_GROUP_DIM = D_MODEL // len(POOL_WINDOWS)
N_EXPERTS = 16
N_EXPERT_GROUPS = 4
EXPERTS_PER_GROUP = 4
EXPERT_FF = 512

LANES = 128
HEAD_PAD = 256
DFT_N1 = 128
MOE_CHUNK = 512
MOE_FF_SPLIT = 2
MIB = 1 << 20


def _cparams(sem, vmem_mib):
    return pltpu.CompilerParams(dimension_semantics=sem, vmem_limit_bytes=vmem_mib * MIB)


def _full(shape):
    nd = len(shape)
    return pl.BlockSpec(shape, lambda *_: (0,) * nd)


def _norm_mod_rows(x, g, sp, sh):
    ms = jnp.mean(x * x, axis=-1, keepdims=True)
    y = x * lax.rsqrt(ms + EPS)
    return (y * g) * sp + sh


def _norm_mod_store(x_ref, g_ref, sc_ref, sh_ref, out_ref, rows, chunk=32):
    g = g_ref[...]
    sp = 1.0 + sc_ref[...]
    sh = sh_ref[...]

    def body(c, carry):
        r = pl.multiple_of(c * chunk, chunk)
        x = x_ref[pl.ds(r, chunk), :]
        out_ref[pl.ds(r, chunk), :] = _norm_mod_rows(x, g, sp, sh).astype(out_ref.dtype)
        return carry

    lax.fori_loop(0, rows // chunk, body, 0)


def _mod_body(c_ref, w_ref, b_ref, o_ref):
    k_dim = c_ref.shape[0]
    tn = o_ref.shape[-1]
    rc = 16

    def body(k, acc):
        r = pl.multiple_of(k * rc, rc)
        c = c_ref[pl.ds(r, rc), :]
        ca = c * jax.nn.sigmoid(c)
        return acc + w_ref[0, pl.ds(r, rc), :] * ca

    acc = lax.fori_loop(0, k_dim // rc, body, jnp.zeros((rc, tn), F32), unroll=4)
    o_ref[0] = jnp.sum(acc, axis=0, keepdims=True) + b_ref[0]


def _ada_mod(c, ada_w, ada_b):
    depth, d, n = ada_w.shape
    tn = 1024
    return pl.pallas_call(
        _mod_body,
        out_shape=jax.ShapeDtypeStruct((depth, 1, n), F32),
        grid=(depth, n // tn),
        in_specs=[
            _full((d, 1)),
            pl.BlockSpec((1, d, tn), lambda i, j: (i, 0, j)),
            pl.BlockSpec((1, 1, tn), lambda i, j: (i, 0, j)),
        ],
        out_specs=pl.BlockSpec((1, 1, tn), lambda i, j: (i, 0, j)),
        compiler_params=_cparams(("arbitrary", "arbitrary"), 40),
        name="ada_mod",
    )(c.reshape(d, 1), ada_w, ada_b.reshape(depth, 1, n))


def _mm_res_body(a_ref, w_ref, b_ref, x_ref, g_ref, o_ref):
    y = jnp.dot(a_ref[...], w_ref[...].astype(BF16), preferred_element_type=F32) + b_ref[...]
    o_ref[...] = x_ref[...] + g_ref[...] * y


def _mm_res(a, w, b, x, gate, tm=1024, tn=512):
    m, k = a.shape
    n = w.shape[1]
    return pl.pallas_call(
        _mm_res_body,
        out_shape=jax.ShapeDtypeStruct((m, n), F32),
        grid=(m // tm, n // tn),
        in_specs=[
            pl.BlockSpec((tm, k), lambda i, j: (i, 0)),
            pl.BlockSpec((k, tn), lambda i, j: (0, j)),
            pl.BlockSpec((1, tn), lambda i, j: (0, j)),
            pl.BlockSpec((tm, tn), lambda i, j: (i, j)),
            pl.BlockSpec((1, tn), lambda i, j: (0, j)),
        ],
        out_specs=pl.BlockSpec((tm, tn), lambda i, j: (i, j)),
        compiler_params=_cparams(("arbitrary", "arbitrary"), 48),
        name="mm_res",
    )(a, w, b, x, gate)


def _conv1_body(x_ref, g_ref, sc_ref, sh_ref, wa_ref, wb_ref, ba_ref, bb_ref, u_ref, h_scr):
    @pl.when(pl.program_id(1) == 0)
    def _():
        _norm_mod_store(x_ref, g_ref, sc_ref, sh_ref, h_scr, h_scr.shape[0])

    h = h_scr[...]
    a = jnp.dot(h, wa_ref[...].astype(BF16), preferred_element_type=F32) + ba_ref[...]
    b = jnp.dot(h, wb_ref[...].astype(BF16), preferred_element_type=F32) + bb_ref[...]
    u_ref[...] = a * jax.nn.sigmoid(b)


def _conv_glu(x, g, sc, sh, w1, b1, tm=1024, tn=512):
    s, d = x.shape
    nb = d // tn
    b1r = b1.reshape(1, 2 * d)
    row = _full((1, d))
    return pl.pallas_call(
        _conv1_body,
        out_shape=jax.ShapeDtypeStruct((s, d), F32),
        grid=(s // tm, nb),
        in_specs=[
            pl.BlockSpec((tm, d), lambda i, j: (i, 0)),
            row, row, row,
            pl.BlockSpec((d, tn), lambda i, j: (0, j)),
            pl.BlockSpec((d, tn), lambda i, j: (0, j + nb)),
            pl.BlockSpec((1, tn), lambda i, j: (0, j)),
            pl.BlockSpec((1, tn), lambda i, j: (0, j + nb)),
        ],
        out_specs=pl.BlockSpec((tm, tn), lambda i, j: (i, j)),
        scratch_shapes=[pltpu.VMEM((tm, d), BF16)],
        compiler_params=_cparams(("arbitrary", "arbitrary"), 56),
        name="conv_glu",
    )(x, g, sc, sh, w1, w1, b1r, b1r)


def _dwconv_body(up_ref, u_ref, un_ref, w_ref, bdw_ref, lg_ref, lb_ref, d_ref, buf, cv, *, ts, cw, ncw):
    i = pl.program_id(0)
    cj = pl.program_id(1)
    ni = pl.num_programs(0)
    halo = 16
    buf[0:halo, :] = jnp.where(i > 0, up_ref[...], 0.0)
    buf[halo:halo + ts, :] = u_ref[...]
    buf[halo + ts:2 * halo + ts, :] = jnp.where(i < ni - 1, un_ref[...], 0.0)
    rc = 32
    for lc in range(cw // LANES):
        ls = slice(lc * LANES, (lc + 1) * LANES)
        wcol = w_ref[:, ls]
        bcol = bdw_ref[:, ls]

        def body(r, carry, ls=ls, wcol=wcol, bcol=bcol):
            r0 = pl.multiple_of(r * rc, rc)
            wrows = rc + 2 * halo
            win = buf[pl.ds(r0, wrows), ls]
            acc = jnp.zeros((rc, LANES), F32)
            for phase in range(8):
                shifted = win if phase == 0 else pltpu.roll(win, wrows - phase, 0)
                for k in range(CONV_WIDTH):
                    off = halo - CONV_PAD + k
                    if off % 8 == phase:
                        acc = acc + shifted[off - phase:off - phase + rc, :] * wcol[k:k + 1, :]
            cv[cj, pl.ds(r0, rc), ls] = acc + bcol
            return carry

        lax.fori_loop(0, ts // rc, body, 0)

    @pl.when(cj == ncw - 1)
    def _():
        d_model = ncw * cw
        rc2 = 32

        def body2(r, carry):
            r0 = pl.multiple_of(r * rc2, rc2)
            parts = [cv[c, pl.ds(r0, rc2), :] for c in range(ncw)]
            tot = parts[0].sum(axis=-1, keepdims=True)
            for p in parts[1:]:
                tot = tot + p.sum(axis=-1, keepdims=True)
            mu = tot / d_model
            cen = [p - mu for p in parts]
            sq = (cen[0] * cen[0]).sum(axis=-1, keepdims=True)
            for p in cen[1:]:
                sq = sq + (p * p).sum(axis=-1, keepdims=True)
            rinv = lax.rsqrt(sq / d_model + EPS)
            for c in range(ncw):
                cs = slice(c * cw, (c + 1) * cw)
                y = (cen[c] * rinv) * lg_ref[:, cs] + lb_ref[:, cs]
                d_ref[pl.ds(r0, rc2), cs] = (y * jax.nn.sigmoid(y)).astype(d_ref.dtype)
            return carry

        lax.fori_loop(0, ts // rc2, body2, 0)


def _dwconv_ln_silu(u, wdw, bdw, ln_g, ln_b, ts=512, cw=512):
    s, d = u.shape
    ncw = d // cw
    hb = ts // 16
    nhb = s // 16
    body = functools.partial(_dwconv_body, ts=ts, cw=cw, ncw=ncw)
    return pl.pallas_call(
        body,
        out_shape=jax.ShapeDtypeStruct((s, d), BF16),
        grid=(s // ts, ncw),
        in_specs=[
            pl.BlockSpec((16, cw), lambda i, j: (jnp.maximum(i * hb - 1, 0), j)),
            pl.BlockSpec((ts, cw), lambda i, j: (i, j)),
            pl.BlockSpec((16, cw), lambda i, j: (jnp.minimum((i + 1) * hb, nhb - 1), j)),
            pl.BlockSpec((CONV_WIDTH, cw), lambda i, j: (0, j)),
            pl.BlockSpec((1, cw), lambda i, j: (0, j)),
            _full((1, d)),
            _full((1, d)),
        ],
        out_specs=pl.BlockSpec((ts, d), lambda i, j: (i, 0)),
        scratch_shapes=[pltpu.VMEM((ts + 32, cw), F32), pltpu.VMEM((ncw, ts, cw), F32)],
        compiler_params=_cparams(("arbitrary", "arbitrary"), 32),
        name="dwconv_ln_silu",
    )(u, u, u, wdw, bdw, ln_g, ln_b)


def _rope_rot(t, cz, sz):
    return t * cz + pltpu.roll(t, QK_ROPE_DIM, 1) * sz


def _mla_down_body(x_ref, g_ref, sc_ref, sh_ref, w_ref, gq_ref, gkv_ref, pos_ref, rc_ref,
                   cq_ref, ckv_ref, kpe_ref, cz_ref, sz_ref, h_scr, w_scr):
    @pl.when(pl.program_id(0) == 0)
    def _():
        w_scr[...] = w_ref[...].astype(BF16)

    _norm_mod_store(x_ref, g_ref, sc_ref, sh_ref, h_scr, h_scr.shape[0])
    down = jnp.dot(h_scr[...], w_scr[...], preferred_element_type=F32)
    cq = down[:, :Q_LORA]
    ckv = down[:, Q_LORA:Q_LORA + KV_LORA]
    cq_ref[...] = (cq * lax.rsqrt(jnp.mean(cq * cq, axis=-1, keepdims=True) + EPS) * gq_ref[...]).astype(BF16)
    ckv_ref[...] = (ckv * lax.rsqrt(jnp.mean(ckv * ckv, axis=-1, keepdims=True) + EPS) * gkv_ref[...]).astype(BF16)
    ang = pos_ref[...].astype(F32) * rc_ref[0:1, :]
    cz = jnp.cos(ang) * rc_ref[1:2, :]
    sz = jnp.sin(ang) * rc_ref[2:3, :]
    cz_ref[...] = cz
    sz_ref[...] = sz
    kpe_ref[...] = _rope_rot(down[:, Q_LORA + KV_LORA:], cz, sz).astype(BF16)


def _mla_down(x, g, sc, sh, w_ext, gq, gkv, pos_col, rope_c, tm=512):
    s, d = x.shape
    n = w_ext.shape[1]
    row = _full((1, d))
    return pl.pallas_call(
        _mla_down_body,
        out_shape=(
            jax.ShapeDtypeStruct((s, Q_LORA), BF16),
            jax.ShapeDtypeStruct((s, KV_LORA), BF16),
            jax.ShapeDtypeStruct((s, LANES), BF16),
            jax.ShapeDtypeStruct((s, LANES), F32),
            jax.ShapeDtypeStruct((s, LANES), F32),
        ),
        grid=(s // tm,),
        in_specs=[
            pl.BlockSpec((tm, d), lambda i: (i, 0)),
            row, row, row,
            _full((d, n)),
            _full((1, Q_LORA)),
            _full((1, KV_LORA)),
            pl.BlockSpec((tm, 1), lambda i: (i, 0)),
            _full((8, LANES)),
        ],
        out_specs=(
            pl.BlockSpec((tm, Q_LORA), lambda i: (i, 0)),
            pl.BlockSpec((tm, KV_LORA), lambda i: (i, 0)),
            pl.BlockSpec((tm, LANES), lambda i: (i, 0)),
            pl.BlockSpec((tm, LANES), lambda i: (i, 0)),
            pl.BlockSpec((tm, LANES), lambda i: (i, 0)),
        ),
        scratch_shapes=[pltpu.VMEM((tm, d), BF16), pltpu.VMEM((d, n), BF16)],
        compiler_params=_cparams(("arbitrary",), 56),
        name="mla_down",
    )(x, g, sc, sh, w_ext, gq, gkv, pos_col, rope_c)


def _mla_up_body(cq_ref, ckv_ref, kpe_ref, cz_ref, sz_ref, wq_ref, wkv_ref, q_ref, k_ref, v_ref, *, qscale):
    qf = jnp.dot(cq_ref[...], wq_ref[0].astype(BF16), preferred_element_type=F32)
    qpe = _rope_rot(qf[:, QK_NOPE_DIM:], cz_ref[...], sz_ref[...])
    q_ref[0] = (jnp.concatenate([qf[:, :QK_NOPE_DIM], qpe], axis=1) * qscale).astype(BF16)
    kv = jnp.dot(ckv_ref[...], wkv_ref[...].astype(BF16), preferred_element_type=F32)
    k_ref[0] = jnp.concatenate([kv[:, :QK_NOPE_DIM].astype(BF16), kpe_ref[...]], axis=1)
    v = kv[:, QK_NOPE_DIM:]
    v_ref[0] = jnp.concatenate([v, jnp.ones_like(v)], axis=1).astype(BF16)


def _mla_up(cq, ckv, kpe, cz, sz, wq_ext, wkv2d, tm=1024):
    s = cq.shape[0]
    qscale = (QK_HEAD_DIM ** -0.5) * math.log2(math.e)
    body = functools.partial(_mla_up_body, qscale=qscale)
    hs = jax.ShapeDtypeStruct((N_HEADS, s, HEAD_PAD), BF16)
    hspec = pl.BlockSpec((1, tm, HEAD_PAD), lambda i, h: (h, i, 0))
    return pl.pallas_call(
        body,
        out_shape=(hs, hs, hs),
        grid=(s // tm, N_HEADS),
        in_specs=[
            pl.BlockSpec((tm, Q_LORA), lambda i, h: (i, 0)),
            pl.BlockSpec((tm, KV_LORA), lambda i, h: (i, 0)),
            pl.BlockSpec((tm, LANES), lambda i, h: (i, 0)),
            pl.BlockSpec((tm, LANES), lambda i, h: (i, 0)),
            pl.BlockSpec((tm, LANES), lambda i, h: (i, 0)),
            pl.BlockSpec((1, Q_LORA, HEAD_PAD), lambda i, h: (h, 0, 0)),
            pl.BlockSpec((KV_LORA, HEAD_PAD), lambda i, h: (0, h)),
        ],
        out_specs=(hspec, hspec, hspec),
        compiler_params=_cparams(("arbitrary", "arbitrary"), 32),
        name="mla_up",
    )(cq, ckv, kpe, cz, sz, wq_ext, wkv2d)


ATTN_SUB_ROWS = 256


def _attn_body(q_ref, k_ref, v_ref, o_ref, m_scr, acc_scr, *, tk, nk):
    tq = q_ref.shape[1]
    m_scr[...] = jnp.full(m_scr.shape, -jnp.inf, F32)
    acc_scr[...] = jnp.zeros(acc_scr.shape, F32)

    def body(c, carry):
        r = pl.multiple_of(c * tk, tk)
        k = k_ref[0, pl.ds(r, tk), :]
        v = v_ref[0, pl.ds(r, tk), :]
        for r0 in range(0, tq, ATTN_SUB_ROWS):
            rows = slice(r0, r0 + ATTN_SUB_ROWS)
            s = lax.dot_general(q_ref[0, rows, :], k, (((1,), (1,)), ((), ())), preferred_element_type=F32)
            m_old = m_scr[rows, :]
            m_new = jnp.maximum(m_old, jnp.max(s, axis=-1, keepdims=True))
            alpha = jnp.exp2(m_old - m_new)
            p = jnp.exp2(s - m_new).astype(BF16)
            acc_scr[rows, :] = acc_scr[rows, :] * alpha + jnp.dot(p, v, preferred_element_type=F32)
            m_scr[rows, :] = m_new
        return carry

    lax.fori_loop(0, nk, body, 0, unroll=2 if nk % 2 == 0 else 1)
    acc = acc_scr[...]
    o_ref[...] = (acc[:, :V_HEAD_DIM] / acc[:, V_HEAD_DIM:V_HEAD_DIM + 1]).astype(o_ref.dtype)


def _attention(q, k, v, tq=1024, tk=1024):
    nh, s, _ = q.shape
    tk = min(tk, s)
    tq = min(tq, s)
    body = functools.partial(_attn_body, tk=tk, nk=s // tk)
    return pl.pallas_call(
        body,
        out_shape=jax.ShapeDtypeStruct((s, nh * V_HEAD_DIM), BF16),
        grid=(nh, s // tq),
        in_specs=[
            pl.BlockSpec((1, tq, HEAD_PAD), lambda h, i: (h, i, 0)),
            pl.BlockSpec((1, s, HEAD_PAD), lambda h, i: (h, 0, 0)),
            pl.BlockSpec((1, s, HEAD_PAD), lambda h, i: (h, 0, 0)),
        ],
        out_specs=pl.BlockSpec((tq, V_HEAD_DIM), lambda h, i: (i, h)),
        scratch_shapes=[pltpu.VMEM((tq, 1), F32), pltpu.VMEM((tq, HEAD_PAD), F32)],
        compiler_params=_cparams(("arbitrary", "arbitrary"), 40),
        name="attention",
    )(q, k, v)


def _dft_tables(s):
    n1 = DFT_N1
    n2 = s // n1
    m = FNET_GROUP_DIM
    c = np.arange(m, dtype=np.float64)
    ang = 2.0 * np.pi * np.outer(c, c) / m
    cs = np.concatenate([np.cos(ang), np.sin(ang)], axis=1)
    a1 = 2.0 * np.pi * np.outer(np.arange(n1), np.arange(n1)) / n1
    c1, s1 = np.cos(a1), np.sin(a1)
    m1 = np.block([[c1, -s1], [-s1, -c1]])
    at = 2.0 * np.pi * np.outer(np.arange(n2), np.arange(n1)) / s
    ct = np.cos(at)[:, :, None]
    st = np.sin(at)[:, :, None]
    a2 = 2.0 * np.pi * np.outer(np.arange(n2), np.arange(n2)) / n2
    scale = 1.0 / math.sqrt(float(s) * m)
    jb = 16
    gmat = np.zeros((n2, jb, 2, n2, jb), dtype=np.float64)
    for j in range(jb):
        gmat[:, j, 0, :, j] = np.cos(a2) * scale
        gmat[:, j, 1, :, j] = np.sin(a2) * scale
    gmat = gmat.reshape(n2 * jb, 2 * n2 * jb)
    return (jnp.asarray(cs, BF16), jnp.asarray(m1, BF16), jnp.asarray(ct, F32), jnp.asarray(st, F32),
            jnp.asarray(gmat, BF16))


def _fnet1_body(x_ref, g_ref, sc_ref, sh_ref, cs_ref, m1_ref, ct_ref, st_ref, o_ref, pq_scr, *, nb):
    n1 = DFT_N1
    g = g_ref[...]
    sp = 1.0 + sc_ref[...]
    sh = sh_ref[...]
    cs = cs_ref[...]
    m1 = m1_ref[...]
    gd = FNET_GROUP_DIM
    for j in range(nb):
        h = _norm_mod_rows(x_ref[:, j, :], g, sp, sh).astype(BF16)
        for gi in range(FNET_GROUPS):
            cols = slice(gi * gd, (gi + 1) * gd)
            pq = jnp.dot(h[:, cols], cs, preferred_element_type=F32)
            pq_scr[0:n1, cols] = pq[:, :gd].astype(BF16)
            pq_scr[n1:2 * n1, cols] = pq[:, gd:].astype(BF16)
        a = jnp.dot(m1, pq_scr[...], preferred_element_type=F32)
        ar = a[:n1]
        ai = a[n1:]
        ct = ct_ref[j]
        st = st_ref[j]
        o_ref[0, j] = (ar * ct + ai * st).astype(BF16)
        o_ref[1, j] = (ai * ct - ar * st).astype(BF16)


def _fnet_stage1(x, g, sc, sh, cs, m1, ct, st, nb=8):
    s, d = x.shape
    n1 = DFT_N1
    n2 = s // n1
    body = functools.partial(_fnet1_body, nb=nb)
    row = _full((1, d))
    return pl.pallas_call(
        body,
        out_shape=jax.ShapeDtypeStruct((2, n2, n1, d), BF16),
        grid=(n2 // nb,),
        in_specs=[
            pl.BlockSpec((n1, nb, d), lambda b: (0, b, 0)),
            row, row, row,
            _full(cs.shape),
            _full(m1.shape),
            pl.BlockSpec((nb, n1, 1), lambda b: (b, 0, 0)),
            pl.BlockSpec((nb, n1, 1), lambda b: (b, 0, 0)),
        ],
        out_specs=pl.BlockSpec((2, nb, n1, d), lambda b: (0, b, 0, 0)),
        scratch_shapes=[pltpu.VMEM((2 * n1, d), BF16)],
        compiler_params=_cparams(("arbitrary",), 48),
        name="fnet_stage1",
    )(x.reshape(n1, n2, d), g, sc, sh, cs, m1, ct, st)


def _fnet2_body(a_ref, g_ref, o_ref):
    blk = a_ref[...]
    rows = blk.shape[0] * blk.shape[1] * blk.shape[2]
    b2 = blk.reshape(rows, blk.shape[3])
    f = jnp.dot(g_ref[...], b2, preferred_element_type=F32)
    o_ref[...] = f.astype(o_ref.dtype).reshape(o_ref.shape)


def _fnet_stage2(a4, gmat, jb=16):
    _, n2, n1, d = a4.shape
    return pl.pallas_call(
        _fnet2_body,
        out_shape=jax.ShapeDtypeStruct((n2, n1, d), BF16),
        grid=(n1 // jb,),
        in_specs=[
            pl.BlockSpec((2, n2, jb, d), lambda b: (0, 0, b, 0)),
            _full(gmat.shape),
        ],
        out_specs=pl.BlockSpec((n2, jb, d), lambda b: (0, b, 0)),
        compiler_params=_cparams(("arbitrary",), 48),
        name="fnet_stage2",
    )(a4, gmat)


def _pool_body(xp_ref, x_ref, xn_ref, g_ref, sc_ref, sh_ref, w_ref, ps_ref, g1_ref, o_ref, hbuf, w_scr, *, ts, seq):
    i = pl.program_id(0)
    ni = pl.num_programs(0)
    halo = 8

    @pl.when(i == 0)
    def _():
        w_scr[...] = w_ref[...].astype(BF16)

    g = g_ref[...]
    sp = 1.0 + sc_ref[...]
    sh = sh_ref[...]
    hbuf[0:halo, :] = jnp.where(i > 0, _norm_mod_rows(xp_ref[...], g, sp, sh), 0.0)
    _norm_mod_store(x_ref, g_ref, sc_ref, sh_ref, hbuf.at[pl.ds(halo, ts), :], ts)
    hbuf[halo + ts:2 * halo + ts, :] = jnp.where(i < ni - 1, _norm_mod_rows(xn_ref[...], g, sp, sh), 0.0)

    t = i * ts + lax.broadcasted_iota(I32, (ts, 1), 0)
    gd = POOL_GROUP_DIM
    for gi, win in enumerate(POOL_WINDOWS):
        cols = slice(gi * gd, (gi + 1) * gd)
        half = win // 2
        wsum = hbuf[halo - half:halo - half + ts, cols]
        for off in range(-half + 1, win - half):
            wsum = wsum + hbuf[halo + off:halo + off + ts, cols]
        count = (jnp.minimum(t + (win - half), seq) - jnp.maximum(t - half, 0)).astype(F32)
        mixed = wsum / count - hbuf[halo:halo + ts, cols]
        y = jnp.dot(mixed.astype(BF16), w_scr[gi], preferred_element_type=F32)
        o_ref[:, cols] = x_ref[:, cols] + g1_ref[:, cols] * (y * ps_ref[:, cols])


def _pool_layer(x, g, sc, sh, w_pool, pscale, g1, ts=512):
    s, d = x.shape
    hb = ts // 8
    nhb = s // 8
    body = functools.partial(_pool_body, ts=ts, seq=s)
    row = _full((1, d))
    return pl.pallas_call(
        body,
        out_shape=jax.ShapeDtypeStruct((s, d), F32),
        grid=(s // ts,),
        in_specs=[
            pl.BlockSpec((8, d), lambda i: (jnp.maximum(i * hb - 1, 0), 0)),
            pl.BlockSpec((ts, d), lambda i: (i, 0)),
            pl.BlockSpec((8, d), lambda i: (jnp.minimum((i + 1) * hb, nhb - 1), 0)),
            row, row, row,
            _full(w_pool.shape),
            row, row,
        ],
        out_specs=pl.BlockSpec((ts, d), lambda i: (i, 0)),
        scratch_shapes=[pltpu.VMEM((ts + 16, d), F32), pltpu.VMEM(w_pool.shape, BF16)],
        compiler_params=_cparams(("arbitrary",), 48),
        name="pool_layer",
    )(x, x, x, g, sc, sh, w_pool, pscale, g1)


SLAB_X = D_MODEL // LANES
SLAB_ROWS = SLAB_X + 8


def _router_body(x_ref, g_ref, sc_ref, sh_ref, rw_ref, rb_ref, grp_ref, slab_ref, h_scr):
    tm = h_scr.shape[0]
    _norm_mod_store(x_ref, g_ref, sc_ref, sh_ref, h_scr, tm)
    h = h_scr[...]
    h_hi = h.astype(BF16)
    h_lo = (h - h_hi.astype(F32)).astype(BF16)
    rw = rw_ref[...]
    rw_hi = rw.astype(BF16)
    rw_lo = (rw - rw_hi.astype(F32)).astype(BF16)
    nt = (((1,), (1,)), ((), ()))
    p_hi = lax.dot_general(jnp.concatenate([rw_hi, rw_lo], axis=0), h_hi, nt, preferred_element_type=F32)
    p_lo = lax.dot_general(rw_hi, h_lo, nt, preferred_element_type=F32)
    logits = p_hi[:N_EXPERTS] + (p_hi[N_EXPERTS:] + p_lo)
    sc = jax.nn.sigmoid(logits)
    sel = sc + rb_ref[...]
    epg = EXPERTS_PER_GROUP
    rows = [sel[e:e + 1, :] for e in range(N_EXPERTS)]
    srow = [sc[e:e + 1, :] for e in range(N_EXPERTS)]
    gscore = []
    for gi in range(N_EXPERT_GROUPS):
        r = rows[gi * epg:(gi + 1) * epg]
        best = None
        for a in range(epg):
            for b in range(a + 1, epg):
                ps = r[a] + r[b]
                best = ps if best is None else jnp.maximum(best, ps)
        gscore.append(best)
    gidx = jnp.zeros_like(gscore[0], dtype=I32)
    gbest = gscore[0]
    for gi in range(1, N_EXPERT_GROUPS):
        better = gscore[gi] > gbest
        gidx = jnp.where(better, gi, gidx)
        gbest = jnp.where(better, gscore[gi], gbest)
    vin = []
    sin_ = []
    for j in range(epg):
        v = rows[j]
        sv = srow[j]
        for gi in range(1, N_EXPERT_GROUPS):
            pick = gidx == gi
            v = jnp.where(pick, rows[gi * epg + j], v)
            sv = jnp.where(pick, srow[gi * epg + j], sv)
        vin.append(v)
        sin_.append(sv)
    l1 = jnp.zeros_like(gidx)
    b1 = vin[0]
    for j in range(1, epg):
        better = vin[j] > b1
        l1 = jnp.where(better, j, l1)
        b1 = jnp.where(better, vin[j], b1)
    neg = jnp.full_like(b1, -jnp.inf)
    l2 = jnp.zeros_like(gidx)
    b2 = neg
    for j in range(epg):
        vj = jnp.where(l1 == j, neg, vin[j])
        better = vj > b2
        l2 = jnp.where(better, j, l2)
        b2 = jnp.where(better, vj, b2)
    w1 = sin_[0]
    w2 = sin_[0]
    for j in range(1, epg):
        w1 = jnp.where(l1 == j, sin_[j], w1)
        w2 = jnp.where(l2 == j, sin_[j], w2)
    tot = w1 + w2
    w1 = w1 / tot
    w2 = w2 / tot
    grp_ref[...] = gidx
    zero = jnp.zeros_like(w1)
    rid = lax.broadcasted_iota(I32, (LANES, tm), 0)
    info = jnp.zeros((LANES, tm), F32)
    for j in range(epg):
        cwj = jnp.where(l1 == j, w1, zero) + jnp.where(l2 == j, w2, zero)
        info = jnp.where(rid == j, cwj, info)
    slab_ref[:, 0:SLAB_X, :] = x_ref[...].reshape(tm, SLAB_X, LANES)
    tail = jnp.concatenate([info.T, jnp.zeros((tm, (SLAB_ROWS - SLAB_X - 1) * LANES), F32)], axis=1)
    slab_ref[:, SLAB_X:SLAB_ROWS, :] = tail.reshape(tm, SLAB_ROWS - SLAB_X, LANES)


def _router(x, g, sc, sh, rw_t, rb_col, tm=512):
    s, d = x.shape
    row = _full((1, d))
    return pl.pallas_call(
        _router_body,
        out_shape=(jax.ShapeDtypeStruct((1, s), I32), jax.ShapeDtypeStruct((s, SLAB_ROWS, LANES), F32)),
        grid=(s // tm,),
        in_specs=[
            pl.BlockSpec((tm, d), lambda i: (i, 0)),
            row, row, row,
            _full((N_EXPERTS, d)),
            _full((N_EXPERTS, 1)),
        ],
        out_specs=(pl.BlockSpec((1, tm), lambda i: (0, i)),
                   pl.BlockSpec((tm, SLAB_ROWS, LANES), lambda i: (i, 0, 0))),
        scratch_shapes=[pltpu.VMEM((tm, d), F32)],
        compiler_params=_cparams(("arbitrary",), 40),
        name="router",
    )(x, g, sc, sh, rw_t, rb_col)


def _invert_body(dest_ref, src_ref):
    n_slots = src_ref.shape[0]
    n_tok = dest_ref.shape[0]

    def clear(i, carry):
        src_ref[i] = 0
        return carry

    lax.fori_loop(0, n_slots, clear, 0)

    def fill(t, carry):
        src_ref[dest_ref[t]] = t
        return carry

    lax.fori_loop(0, n_tok, fill, 0)


def _invert(dest, n_slots):
    return pl.pallas_call(
        _invert_body,
        out_shape=jax.ShapeDtypeStruct((n_slots,), I32),
        grid_spec=pltpu.PrefetchScalarGridSpec(
            num_scalar_prefetch=1,
            grid=(1,),
            in_specs=[],
            out_specs=pl.BlockSpec(memory_space=pltpu.SMEM),
        ),
        compiler_params=_cparams(("arbitrary",), 16),
        name="moe_invert",
    )(dest)


def _slab_to_rows_body(x3_ref, o_ref):
    o_ref[...] = x3_ref[...].reshape(o_ref.shape)


def _slab_to_rows(x3, tm=512):
    s = x3.shape[0]
    d = SLAB_X * LANES
    return pl.pallas_call(
        _slab_to_rows_body,
        out_shape=jax.ShapeDtypeStruct((s, d), F32),
        grid=(s // tm,),
        in_specs=[pl.BlockSpec((tm, SLAB_X, LANES), lambda i: (i, 0, 0))],
        out_specs=pl.BlockSpec((tm, d), lambda i: (i, 0)),
        compiler_params=_cparams(("arbitrary",), 32),
        name="slab_to_rows",
    )(x3)


def _moe_body(gid_ref, nu_ref, nv_ref, src_ref, slab_hbm, g_ref, sc_ref, sh_ref, g2_ref, wg_ref, wu_ref, wd_ref,
              out_hbm, xbuf, obuf, h_scr, acc_scr, act_scr, gsem, ssem, *, nslot):
    c = pl.program_id(0)
    sl = pl.program_id(1)
    nused = nu_ref[0]
    used = c < nused
    rows = h_scr.shape[0]
    per = rows // nslot
    buf = c % 2

    def gather_copy(chunk, r, b):
        tok = src_ref[chunk * rows + r]
        return pltpu.make_async_copy(slab_hbm.at[pl.ds(tok, 1)], xbuf.at[b, pl.ds(r, 1)], gsem.at[b])

    def start_gather(chunk, part, b):
        def body(i, carry):
            gather_copy(chunk, part * per + i, b).start()
            return carry

        lax.fori_loop(0, per, body, 0)

    def scatter_copy(chunk, r):
        tok = src_ref[chunk * rows + r]
        return pltpu.make_async_copy(obuf.at[pl.ds(r, 1)], out_hbm.at[pl.ds(tok, 1)], ssem.at[0])

    def wait_scatter(chunk):
        def body(r, carry):
            scatter_copy(chunk, r).wait()
            return carry

        lax.fori_loop(0, nv_ref[chunk], body, 0)

    @pl.when(jnp.logical_and(c == 0, sl == 0))
    def _():
        for part in range(nslot):
            start_gather(0, part, 0)

    @pl.when(jnp.logical_and(used, sl == 0))
    def _():
        def wbody(r, carry):
            gather_copy(c, r, buf).wait()
            return carry

        lax.fori_loop(0, rows, wbody, 0)

    @pl.when(c + 1 < nused)
    def _():
        start_gather(c + 1, sl, 1 - buf)

    @pl.when(jnp.logical_and(used, sl == 0))
    def _():
        g = g_ref[...]
        sp = 1.0 + sc_ref[...]
        sh = sh_ref[...]
        chunk = 32

        def body(k, carry):
            r = pl.multiple_of(k * chunk, chunk)
            x = xbuf[buf, pl.ds(r, chunk), 0:SLAB_X, :].reshape(chunk, SLAB_X * LANES)
            h_scr[pl.ds(r, chunk), :] = _norm_mod_rows(x, g, sp, sh).astype(BF16)
            return carry

        lax.fori_loop(0, rows // chunk, body, 0)
        acc_scr[...] = jnp.zeros(acc_scr.shape, F32)

    @pl.when(used)
    def _():
        h = h_scr[...]
        gate = jnp.dot(h, wg_ref[0, 0].astype(BF16), preferred_element_type=F32)
        up = jnp.dot(h, wu_ref[0, 0].astype(BF16), preferred_element_type=F32)
        j = sl // MOE_FF_SPLIT
        info = xbuf[buf, :, SLAB_X:SLAB_ROWS, :].reshape(rows, (SLAB_ROWS - SLAB_X) * LANES)[:, :LANES]
        cw = jnp.zeros((rows, 1), F32)
        for e in range(EXPERTS_PER_GROUP):
            cw = jnp.where(j == e, info[:, e:e + 1], cw)
        act = (gate * jax.nn.sigmoid(gate)) * up * cw
        part = sl % MOE_FF_SPLIT
        act_scr[part] = act.astype(BF16)

        @pl.when(part == MOE_FF_SPLIT - 1)
        def _():
            full = jnp.concatenate([act_scr[k] for k in range(MOE_FF_SPLIT)], axis=1)
            acc_scr[...] += jnp.dot(full, wd_ref[0, 0].astype(BF16), preferred_element_type=F32)

    @pl.when(jnp.logical_and(used, sl == nslot - 1))
    def _():
        @pl.when(c > 0)
        def _():
            wait_scatter(c - 1)

        y = g2_ref[...] * acc_scr[...]
        obuf[...] = xbuf[buf, :, 0:SLAB_X, :] + y.reshape(rows, SLAB_X, LANES)

        def sbody(r, carry):
            scatter_copy(c, r).start()
            return carry

        lax.fori_loop(0, nv_ref[c], sbody, 0)

        @pl.when(c == nused - 1)
        def _():
            wait_scatter(c)


def _moe_experts(gid, nused, nvalid, src, slabs, g, sc, sh, g2, w_gate, w_up, w_down, layer):
    s = slabs.shape[0]
    d = SLAB_X * LANES
    cm = MOE_CHUNK
    nc = src.shape[0] // cm
    nslot = EXPERTS_PER_GROUP * MOE_FF_SPLIT
    fs = EXPERT_FF // MOE_FF_SPLIT

    def eidx(c, sl, gid_ref, nu_ref):
        last = nu_ref[0] - 1
        cc = jnp.minimum(c, last)
        used = c <= last
        e = EXPERTS_PER_GROUP * gid_ref[cc] + jnp.where(used, sl // MOE_FF_SPLIT, EXPERTS_PER_GROUP - 1)
        half = jnp.where(used, sl % MOE_FF_SPLIT, MOE_FF_SPLIT - 1)
        return e, half

    def wgmap(c, sl, gid_ref, nu_ref, nv_ref, src_ref):
        e, half = eidx(c, sl, gid_ref, nu_ref)
        return (layer, e, 0, half)

    def wdmap(c, sl, gid_ref, nu_ref, nv_ref, src_ref):
        e, _ = eidx(c, sl, gid_ref, nu_ref)
        return (layer, e, 0, 0)

    row = pl.BlockSpec((1, d), lambda c, sl, *_: (0, 0))
    anyspec = pl.BlockSpec(memory_space=pl.ANY)
    body = functools.partial(_moe_body, nslot=nslot)
    return pl.pallas_call(
        body,
        out_shape=jax.ShapeDtypeStruct((s, SLAB_X, LANES), F32),
        grid_spec=pltpu.PrefetchScalarGridSpec(
            num_scalar_prefetch=4,
            grid=(nc, nslot),
            in_specs=[
                anyspec,
                row, row, row, row,
                pl.BlockSpec((1, 1, d, fs), wgmap),
                pl.BlockSpec((1, 1, d, fs), wgmap),
                pl.BlockSpec((1, 1, EXPERT_FF, d), wdmap),
            ],
            out_specs=anyspec,
            scratch_shapes=[
                pltpu.VMEM((2, cm, SLAB_ROWS, LANES), F32),
                pltpu.VMEM((cm, SLAB_X, LANES), F32),
                pltpu.VMEM((cm, d), BF16),
                pltpu.VMEM((cm, d), F32),
                pltpu.VMEM((MOE_FF_SPLIT, cm, fs), BF16),
                pltpu.SemaphoreType.DMA((2,)),
                pltpu.SemaphoreType.DMA((1,)),
            ],
        ),
        compiler_params=_cparams(("arbitrary", "arbitrary"), 56),
        name="moe_experts",
    )(gid, nused, nvalid, src, slabs, g, sc, sh, g2, w_gate, w_up, w_down)


def _dispatch_tables(grp, nc):
    cm = MOE_CHUNK
    ng = N_EXPERT_GROUPS
    oh = (grp[:, None] == jnp.arange(ng, dtype=I32)[None, :]).astype(I32)
    cs = jnp.cumsum(oh, axis=0)
    counts = cs[-1]
    rank = jnp.sum(cs * oh, axis=1) - 1
    nch = (counts + cm - 1) // cm
    cum = jnp.cumsum(nch)
    first = cum - nch
    dest = jnp.sum(oh * (first * cm)[None, :], axis=1) + rank
    nused = cum[-1:]
    cidx = jnp.arange(nc, dtype=I32)
    gid = jnp.minimum(jnp.sum((cidx[:, None] >= cum[None, :]).astype(I32), axis=1), ng - 1)
    goh = (gid[:, None] == jnp.arange(ng, dtype=I32)[None, :]).astype(I32)
    left = jnp.sum(goh * counts[None, :], axis=1) - (cidx - jnp.sum(goh * first[None, :], axis=1)) * cm
    nvalid = jnp.where(cidx < cum[-1], jnp.clip(left, 0, cm), 0)
    return dest.astype(I32), gid.astype(I32), nused.astype(I32), nvalid.astype(I32)


def _moe_layer(x, g, sc, sh, g2, rw_t, rb_col, w_gate, w_up, w_down, layer):
    s, d = x.shape
    nc = s // MOE_CHUNK + N_EXPERT_GROUPS
    grp, slabs = _router(x, g, sc, sh, rw_t, rb_col)
    dest, gid, nused, nvalid = _dispatch_tables(grp[0], nc)
    src = _invert(dest, nc * MOE_CHUNK)
    y3 = _moe_experts(gid, nused, nvalid, src, slabs, g, sc, sh, g2, w_gate, w_up, w_down, layer)
    return _slab_to_rows(y3)


def _final_body(x_ref, g_ref, o_ref):
    x = x_ref[...]
    ms = jnp.mean(x * x, axis=-1, keepdims=True)
    o_ref[...] = (x * lax.rsqrt(ms + EPS)) * g_ref[...]


def _final_norm(x, g, tm=256):
    s, d = x.shape
    return pl.pallas_call(
        _final_body,
        out_shape=jax.ShapeDtypeStruct((s, d), F32),
        grid=(s // tm,),
        in_specs=[pl.BlockSpec((tm, d), lambda i: (i, 0)), _full((1, d))],
        out_specs=pl.BlockSpec((tm, d), lambda i: (i, 0)),
        compiler_params=_cparams(("arbitrary",), 32),
        name="final_norm",
    )(x, g)


def _swap_halves(w):
    half = w.shape[-1] // 2
    return jnp.concatenate([w[..., half:], w[..., :half]], axis=-1)


def kernel(x, c, positions, ada_w, ada_b, norm_g, final_g, conv_w1, conv_b1, conv_wdw, conv_bdw, conv_ln_g, conv_ln_b, conv_w2, conv_b2, mla_w_down, mla_g_q, mla_g_kv, mla_w_uq, mla_w_ukv, mla_w_o, fnet_w, fnet_b, pool_w, pool_scale, router_w, router_b, moe_w_gate, moe_w_up, moe_w_down):
    b, s, d = x.shape
    assert b == 1 and d == D_MODEL
    xs = x.reshape(s, d)
    mod = _ada_mod(c, ada_w, ada_b).reshape(DEPTH, 6, 1, d)
    rw_t = router_w.T
    rb_col = router_b.reshape(N_EXPERTS, 1)
    zero_bias = jnp.zeros((1, d), F32)

    for i in range(DEPTH):
        sh1, sc1, g1, sh2, sc2, g2 = [mod[i, k] for k in range(6)]
        ng1 = norm_g[i, 0].reshape(1, d)
        ng2 = norm_g[i, 1].reshape(1, d)
        kind, j = i % 4, i // 4
        if kind == 0:
            u = _conv_glu(xs, ng1, sc1, sh1, conv_w1[j], conv_b1[j])
            dd = _dwconv_ln_silu(u, conv_wdw[j].reshape(CONV_WIDTH, d), conv_bdw[j].reshape(1, d),
                                 conv_ln_g[j].reshape(1, d), conv_ln_b[j].reshape(1, d))
            xs = _mm_res(dd, conv_w2[j], conv_b2[j].reshape(1, d), xs, g1)
        elif kind == 1:
            wd = mla_w_down[j]
            w_ext = jnp.concatenate([wd, _swap_halves(wd[:, Q_LORA + KV_LORA:])], axis=1)
            inv = ROPE_THETA ** (-jnp.arange(0, QK_ROPE_DIM, 2, dtype=F32) / QK_ROPE_DIM)
            zeros64 = jnp.zeros((QK_ROPE_DIM,), F32)
            half = QK_ROPE_DIM // 2
            rope_c = jnp.zeros((8, LANES), F32)
            rope_c = rope_c.at[0].set(jnp.concatenate([inv, inv, zeros64]))
            rope_c = rope_c.at[1].set(jnp.concatenate([jnp.ones((QK_ROPE_DIM,), F32), zeros64]))
            rope_c = rope_c.at[2].set(jnp.concatenate([-jnp.ones((half,), F32), jnp.ones((half,), F32), zeros64]))
            cq, ckv, kpe, cz, sz = _mla_down(xs, ng1, sc1, sh1, w_ext, mla_g_q[j].reshape(1, Q_LORA),
                                             mla_g_kv[j].reshape(1, KV_LORA), positions.reshape(s, 1), rope_c)
            wq = mla_w_uq[j]
            wq_ext = jnp.concatenate([wq, _swap_halves(wq[..., QK_NOPE_DIM:])], axis=-1)
            wq_ext = jnp.transpose(wq_ext, (1, 0, 2))
            wkv2d = mla_w_ukv[j].reshape(KV_LORA, N_HEADS * (QK_NOPE_DIM + V_HEAD_DIM))
            qh, kh, vh = _mla_up(cq, ckv, kpe, cz, sz, wq_ext, wkv2d)
            o = _attention(qh, kh, vh)
            xs = _mm_res(o, mla_w_o[j].reshape(N_HEADS * V_HEAD_DIM, d), zero_bias, xs, g1)
        elif kind == 2:
            cs, m1, ct, st, gmat = _dft_tables(s)
            a4 = _fnet_stage1(xs, ng1, sc1, sh1, cs, m1, ct, st)
            f = _fnet_stage2(a4, gmat)
            xs = _mm_res(f.reshape(s, d), fnet_w[j], fnet_b[j].reshape(1, d), xs, g1)
        else:
            xs = _pool_layer(xs, ng1, sc1, sh1, pool_w[j], pool_scale[j].reshape(1, d), g1)
        xs = _moe_layer(xs, ng2, sc2, sh2, g2, rw_t, rb_col, moe_w_gate, moe_w_up, moe_w_down, i)

    return _final_norm(xs, final_g.reshape(1, d)).reshape(b, s, d)
```

```python
import functools
import math

import numpy as np
import jax
import jax.numpy as jnp
from jax import lax
from jax.experimental import pallas as pl
from jax.experimental.pallas import tpu as pltpu

F32 = jnp.float32
BF16 = jnp.bfloat16
I32 = jnp.int32

D_MODEL = 2048
DEPTH = 4
EPS = 1e-6
CONV_WIDTH = 31
CONV_PAD = CONV_WIDTH // 2
N_HEADS = 16
Q_LORA = 512
KV_LORA = 512
QK_NOPE_DIM = 128
QK_ROPE_DIM = 64
V_HEAD_DIM = 128
QK_HEAD_DIM = QK_NOPE_DIM + QK_ROPE_DIM
ROPE_THETA = 10000.0
FNET_GROUPS = 4
FNET_GROUP_DIM = D_MODEL // FNET_GROUPS
POOL_WINDOWS = (2, 4, 8, 16)
POOL_GROUP_DIM = D_MODEL // len(POOL_WINDOWS)
N_EXPERTS = 16
N_EXPERT_GROUPS = 4
EXPERTS_PER_GROUP = 4
EXPERT_FF = 512

LANES = 128
HEAD_PAD = 256
DFT_N1 = 128
MOE_CHUNK = 512
MOE_FF_SPLIT = 2
MIB = 1 << 20


def _cparams(sem, vmem_mib):
    return pltpu.CompilerParams(dimension_semantics=sem, vmem_limit_bytes=vmem_mib * MIB)


def _full(shape):
    nd = len(shape)
    return pl.BlockSpec(shape, lambda *_: (0,) * nd)


def _norm_mod_rows(x, g, sp, sh):
    ms = jnp.mean(x * x, axis=-1, keepdims=True)
    y = x * lax.rsqrt(ms + EPS)
    return (y * g) * sp + sh


def _norm_mod_store(x_ref, g_ref, sc_ref, sh_ref, out_ref, rows, chunk=32):
    g = g_ref[...]
    sp = 1.0 + sc_ref[...]
    sh = sh_ref[...]

    def body(c, carry):
        r = pl.multiple_of(c * chunk, chunk)
        x = x_ref[pl.ds(r, chunk), :]
        out_ref[pl.ds(r, chunk), :] = _norm_mod_rows(x, g, sp, sh).astype(out_ref.dtype)
        return carry

    lax.fori_loop(0, rows // chunk, body, 0)


def _mod_body(c_ref, w_ref, b_ref, o_ref):
    k_dim = c_ref.shape[0]
    tn = o_ref.shape[-1]
    rc = 16

    def body(k, acc):
        r = pl.multiple_of(k * rc, rc)
        c = c_ref[pl.ds(r, rc), :]
        ca = c * jax.nn.sigmoid(c)
        return acc + w_ref[0, pl.ds(r, rc), :] * ca

    acc = lax.fori_loop(0, k_dim // rc, body, jnp.zeros((rc, tn), F32), unroll=4)
    o_ref[0] = jnp.sum(acc, axis=0, keepdims=True) + b_ref[0]


def _ada_mod(c, ada_w, ada_b):
    depth, d, n = ada_w.shape
    tn = 1024
    return pl.pallas_call(
        _mod_body,
        out_shape=jax.ShapeDtypeStruct((depth, 1, n), F32),
        grid=(depth, n // tn),
        in_specs=[
            _full((d, 1)),
            pl.BlockSpec((1, d, tn), lambda i, j: (i, 0, j)),
            pl.BlockSpec((1, 1, tn), lambda i, j: (i, 0, j)),
        ],
        out_specs=pl.BlockSpec((1, 1, tn), lambda i, j: (i, 0, j)),
        compiler_params=_cparams(("arbitrary", "arbitrary"), 40),
        name="ada_mod",
    )(c.reshape(d, 1), ada_w, ada_b.reshape(depth, 1, n))


def _mm_res_body(a_ref, w_ref, b_ref, x_ref, g_ref, o_ref):
    y = jnp.dot(a_ref[...], w_ref[...].astype(BF16), preferred_element_type=F32) + b_ref[...]
    o_ref[...] = x_ref[...] + g_ref[...] * y


def _mm_res(a, w, b, x, gate, tm=1024, tn=512):
    m, k = a.shape
    n = w.shape[1]
    return pl.pallas_call(
        _mm_res_body,
        out_shape=jax.ShapeDtypeStruct((m, n), F32),
        grid=(m // tm, n // tn),
        in_specs=[
            pl.BlockSpec((tm, k), lambda i, j: (i, 0)),
            pl.BlockSpec((k, tn), lambda i, j: (0, j)),
            pl.BlockSpec((1, tn), lambda i, j: (0, j)),
            pl.BlockSpec((tm, tn), lambda i, j: (i, j)),
            pl.BlockSpec((1, tn), lambda i, j: (0, j)),
        ],
        out_specs=pl.BlockSpec((tm, tn), lambda i, j: (i, j)),
        compiler_params=_cparams(("arbitrary", "arbitrary"), 48),
        name="mm_res",
    )(a, w, b, x, gate)


def _conv1_body(x_ref, g_ref, sc_ref, sh_ref, wa_ref, wb_ref, ba_ref, bb_ref, u_ref, h_scr):
    @pl.when(pl.program_id(1) == 0)
    def _():
        _norm_mod_store(x_ref, g_ref, sc_ref, sh_ref, h_scr, h_scr.shape[0])

    h = h_scr[...]
    a = jnp.dot(h, wa_ref[...].astype(BF16), preferred_element_type=F32) + ba_ref[...]
    b = jnp.dot(h, wb_ref[...].astype(BF16), preferred_element_type=F32) + bb_ref[...]
    u_ref[...] = a * jax.nn.sigmoid(b)


def _conv_glu(x, g, sc, sh, w1, b1, tm=1024, tn=512):
    s, d = x.shape
    nb = d // tn
    b1r = b1.reshape(1, 2 * d)
    row = _full((1, d))
    return pl.pallas_call(
        _conv1_body,
        out_shape=jax.ShapeDtypeStruct((s, d), F32),
        grid=(s // tm, nb),
        in_specs=[
            pl.BlockSpec((tm, d), lambda i, j: (i, 0)),
            row, row, row,
            pl.BlockSpec((d, tn), lambda i, j: (0, j)),
            pl.BlockSpec((d, tn), lambda i, j: (0, j + nb)),
            pl.BlockSpec((1, tn), lambda i, j: (0, j)),
            pl.BlockSpec((1, tn), lambda i, j: (0, j + nb)),
        ],
        out_specs=pl.BlockSpec((tm, tn), lambda i, j: (i, j)),
        scratch_shapes=[pltpu.VMEM((tm, d), BF16)],
        compiler_params=_cparams(("arbitrary", "arbitrary"), 56),
        name="conv_glu",
    )(x, g, sc, sh, w1, w1, b1r, b1r)


def _dwconv_body(up_ref, u_ref, un_ref, w_ref, bdw_ref, lg_ref, lb_ref, d_ref, buf, cv, *, ts, cw, ncw):
    i = pl.program_id(0)
    cj = pl.program_id(1)
    ni = pl.num_programs(0)
    halo = 16
    buf[0:halo, :] = jnp.where(i > 0, up_ref[...], 0.0)
    buf[halo:halo + ts, :] = u_ref[...]
    buf[halo + ts:2 * halo + ts, :] = jnp.where(i < ni - 1, un_ref[...], 0.0)
    rc = 32
    for lc in range(cw // LANES):
        ls = slice(lc * LANES, (lc + 1) * LANES)
        wcol = w_ref[:, ls]
        bcol = bdw_ref[:, ls]

        def body(r, carry, ls=ls, wcol=wcol, bcol=bcol):
            r0 = pl.multiple_of(r * rc, rc)
            wrows = rc + 2 * halo
            win = buf[pl.ds(r0, wrows), ls]
            acc = jnp.zeros((rc, LANES), F32)
            for phase in range(8):
                shifted = win if phase == 0 else pltpu.roll(win, wrows - phase, 0)
                for k in range(CONV_WIDTH):
                    off = halo - CONV_PAD + k
                    if off % 8 == phase:
                        acc = acc + shifted[off - phase:off - phase + rc, :] * wcol[k:k + 1, :]
            cv[cj, pl.ds(r0, rc), ls] = acc + bcol
            return carry

        lax.fori_loop(0, ts // rc, body, 0)

    @pl.when(cj == ncw - 1)
    def _():
        d_model = ncw * cw
        rc2 = 32

        def body2(r, carry):
            r0 = pl.multiple_of(r * rc2, rc2)
            parts = [cv[c, pl.ds(r0, rc2), :] for c in range(ncw)]
            tot = parts[0].sum(axis=-1, keepdims=True)
            for p in parts[1:]:
                tot = tot + p.sum(axis=-1, keepdims=True)
            mu = tot / d_model
            cen = [p - mu for p in parts]
            sq = (cen[0] * cen[0]).sum(axis=-1, keepdims=True)
            for p in cen[1:]:
                sq = sq + (p * p).sum(axis=-1, keepdims=True)
            rinv = lax.rsqrt(sq / d_model + EPS)
            for c in range(ncw):
                cs = slice(c * cw, (c + 1) * cw)
                y = (cen[c] * rinv) * lg_ref[:, cs] + lb_ref[:, cs]
                d_ref[pl.ds(r0, rc2), cs] = (y * jax.nn.sigmoid(y)).astype(d_ref.dtype)
            return carry

        lax.fori_loop(0, ts // rc2, body2, 0)


def _dwconv_ln_silu(u, wdw, bdw, ln_g, ln_b, ts=512, cw=512):
    s, d = u.shape
    ncw = d // cw
    hb = ts // 16
    nhb = s // 16
    body = functools.partial(_dwconv_body, ts=ts, cw=cw, ncw=ncw)
    return pl.pallas_call(
        body,
        out_shape=jax.ShapeDtypeStruct((s, d), BF16),
        grid=(s // ts, ncw),
        in_specs=[
            pl.BlockSpec((16, cw), lambda i, j: (jnp.maximum(i * hb - 1, 0), j)),
            pl.BlockSpec((ts, cw), lambda i, j: (i, j)),
            pl.BlockSpec((16, cw), lambda i, j: (jnp.minimum((i + 1) * hb, nhb - 1), j)),
            pl.BlockSpec((CONV_WIDTH, cw), lambda i, j: (0, j)),
            pl.BlockSpec((1, cw), lambda i, j: (0, j)),
            _full((1, d)),
            _full((1, d)),
        ],
        out_specs=pl.BlockSpec((ts, d), lambda i, j: (i, 0)),
        scratch_shapes=[pltpu.VMEM((ts + 32, cw), F32), pltpu.VMEM((ncw, ts, cw), F32)],
        compiler_params=_cparams(("arbitrary", "arbitrary"), 32),
        name="dwconv_ln_silu",
    )(u, u, u, wdw, bdw, ln_g, ln_b)


def _rope_rot(t, cz, sz):
    return t * cz + pltpu.roll(t, QK_ROPE_DIM, 1) * sz


def _mla_down_body(x_ref, g_ref, sc_ref, sh_ref, w_ref, gq_ref, gkv_ref, pos_ref, rc_ref,
                   cq_ref, ckv_ref, kpe_ref, cz_ref, sz_ref, h_scr, w_scr):
    @pl.when(pl.program_id(0) == 0)
    def _():
        w_scr[...] = w_ref[...].astype(BF16)

    _norm_mod_store(x_ref, g_ref, sc_ref, sh_ref, h_scr, h_scr.shape[0])
    down = jnp.dot(h_scr[...], w_scr[...], preferred_element_type=F32)
    cq = down[:, :Q_LORA]
    ckv = down[:, Q_LORA:Q_LORA + KV_LORA]
    cq_ref[...] = (cq * lax.rsqrt(jnp.mean(cq * cq, axis=-1, keepdims=True) + EPS) * gq_ref[...]).astype(BF16)
    ckv_ref[...] = (ckv * lax.rsqrt(jnp.mean(ckv * ckv, axis=-1, keepdims=True) + EPS) * gkv_ref[...]).astype(BF16)
    ang = pos_ref[...].astype(F32) * rc_ref[0:1, :]
    cz = jnp.cos(ang) * rc_ref[1:2, :]
    sz = jnp.sin(ang) * rc_ref[2:3, :]
    cz_ref[...] = cz
    sz_ref[...] = sz
    kpe_ref[...] = _rope_rot(down[:, Q_LORA + KV_LORA:], cz, sz).astype(BF16)


def _mla_down(x, g, sc, sh, w_ext, gq, gkv, pos_col, rope_c, tm=512):
    s, d = x.shape
    n = w_ext.shape[1]
    row = _full((1, d))
    return pl.pallas_call(
        _mla_down_body,
        out_shape=(
            jax.ShapeDtypeStruct((s, Q_LORA), BF16),
            jax.ShapeDtypeStruct((s, KV_LORA), BF16),
            jax.ShapeDtypeStruct((s, LANES), BF16),
            jax.ShapeDtypeStruct((s, LANES), F32),
            jax.ShapeDtypeStruct((s, LANES), F32),
        ),
        grid=(s // tm,),
        in_specs=[
            pl.BlockSpec((tm, d), lambda i: (i, 0)),
            row, row, row,
            _full((d, n)),
            _full((1, Q_LORA)),
            _full((1, KV_LORA)),
            pl.BlockSpec((tm, 1), lambda i: (i, 0)),
            _full((8, LANES)),
        ],
        out_specs=(
            pl.BlockSpec((tm, Q_LORA), lambda i: (i, 0)),
            pl.BlockSpec((tm, KV_LORA), lambda i: (i, 0)),
            pl.BlockSpec((tm, LANES), lambda i: (i, 0)),
            pl.BlockSpec((tm, LANES), lambda i: (i, 0)),
            pl.BlockSpec((tm, LANES), lambda i: (i, 0)),
        ),
        scratch_shapes=[pltpu.VMEM((tm, d), BF16), pltpu.VMEM((d, n), BF16)],
        compiler_params=_cparams(("arbitrary",), 56),
        name="mla_down",
    )(x, g, sc, sh, w_ext, gq, gkv, pos_col, rope_c)


def _mla_up_body(cq_ref, ckv_ref, kpe_ref, cz_ref, sz_ref, wq_ref, wkv_ref, q_ref, k_ref, v_ref, *, qscale):
    qf = jnp.dot(cq_ref[...], wq_ref[0].astype(BF16), preferred_element_type=F32)
    qpe = _rope_rot(qf[:, QK_NOPE_DIM:], cz_ref[...], sz_ref[...])
    q_ref[0] = (jnp.concatenate([qf[:, :QK_NOPE_DIM], qpe], axis=1) * qscale).astype(BF16)
    kv = jnp.dot(ckv_ref[...], wkv_ref[...].astype(BF16), preferred_element_type=F32)
    k_ref[0] = jnp.concatenate([kv[:, :QK_NOPE_DIM].astype(BF16), kpe_ref[...]], axis=1)
    v = kv[:, QK_NOPE_DIM:]
    v_ref[0] = jnp.concatenate([v, jnp.ones_like(v)], axis=1).astype(BF16)


def _mla_up(cq, ckv, kpe, cz, sz, wq_ext, wkv2d, tm=1024):
    s = cq.shape[0]
    qscale = (QK_HEAD_DIM ** -0.5) * math.log2(math.e)
    body = functools.partial(_mla_up_body, qscale=qscale)
    hs = jax.ShapeDtypeStruct((N_HEADS, s, HEAD_PAD), BF16)
    hspec = pl.BlockSpec((1, tm, HEAD_PAD), lambda i, h: (h, i, 0))
    return pl.pallas_call(
        body,
        out_shape=(hs, hs, hs),
        grid=(s // tm, N_HEADS),
        in_specs=[
            pl.BlockSpec((tm, Q_LORA), lambda i, h: (i, 0)),
            pl.BlockSpec((tm, KV_LORA), lambda i, h: (i, 0)),
            pl.BlockSpec((tm, LANES), lambda i, h: (i, 0)),
            pl.BlockSpec((tm, LANES), lambda i, h: (i, 0)),
            pl.BlockSpec((tm, LANES), lambda i, h: (i, 0)),
            pl.BlockSpec((1, Q_LORA, HEAD_PAD), lambda i, h: (h, 0, 0)),
            pl.BlockSpec((KV_LORA, HEAD_PAD), lambda i, h: (0, h)),
        ],
        out_specs=(hspec, hspec, hspec),
        compiler_params=_cparams(("arbitrary", "arbitrary"), 32),
        name="mla_up",
    )(cq, ckv, kpe, cz, sz, wq_ext, wkv2d)


ATTN_SUB_ROWS = 256


def _attn_body(q_ref, k_ref, v_ref, o_ref, m_scr, acc_scr, *, tk, nk):
    tq = q_ref.shape[1]
    m_scr[...] = jnp.full(m_scr.shape, -jnp.inf, F32)
    acc_scr[...] = jnp.zeros(acc_scr.shape, F32)

    def body(c, carry):
        r = pl.multiple_of(c * tk, tk)
        k = k_ref[0, pl.ds(r, tk), :]
        v = v_ref[0, pl.ds(r, tk), :]
        for r0 in range(0, tq, ATTN_SUB_ROWS):
            rows = slice(r0, r0 + ATTN_SUB_ROWS)
            s = lax.dot_general(q_ref[0, rows, :], k, (((1,), (1,)), ((), ())), preferred_element_type=F32)
            m_old = m_scr[rows, :]
            m_new = jnp.maximum(m_old, jnp.max(s, axis=-1, keepdims=True))
            alpha = jnp.exp2(m_old - m_new)
            p = jnp.exp2(s - m_new).astype(BF16)
            acc_scr[rows, :] = acc_scr[rows, :] * alpha + jnp.dot(p, v, preferred_element_type=F32)
            m_scr[rows, :] = m_new
        return carry

    lax.fori_loop(0, nk, body, 0, unroll=2 if nk % 2 == 0 else 1)
    acc = acc_scr[...]
    o_ref[...] = (acc[:, :V_HEAD_DIM] / acc[:, V_HEAD_DIM:V_HEAD_DIM + 1]).astype(o_ref.dtype)


def _attention(q, k, v, tq=2048, tk=1024):
    nh, s, _ = q.shape
    tk = min(tk, s)
    tq = min(tq, s)
    body = functools.partial(_attn_body, tk=tk, nk=s // tk)
    return pl.pallas_call(
        body,
        out_shape=jax.ShapeDtypeStruct((s, nh * V_HEAD_DIM), BF16),
        grid=(nh, s // tq),
        in_specs=[
            pl.BlockSpec((1, tq, HEAD_PAD), lambda h, i: (h, i, 0)),
            pl.BlockSpec((1, s, HEAD_PAD), lambda h, i: (h, 0, 0)),
            pl.BlockSpec((1, s, HEAD_PAD), lambda h, i: (h, 0, 0)),
        ],
        out_specs=pl.BlockSpec((tq, V_HEAD_DIM), lambda h, i: (i, h)),
        scratch_shapes=[pltpu.VMEM((tq, 1), F32), pltpu.VMEM((tq, HEAD_PAD), F32)],
        compiler_params=_cparams(("arbitrary", "arbitrary"), 40),
        name="attention",
    )(q, k, v)


def _dft_tables(s):
    n1 = DFT_N1
    n2 = s // n1
    m = FNET_GROUP_DIM
    c = np.arange(m, dtype=np.float64)
    ang = 2.0 * np.pi * np.outer(c, c) / m
    cs = np.concatenate([np.cos(ang), np.sin(ang)], axis=1)
    a1 = 2.0 * np.pi * np.outer(np.arange(n1), np.arange(n1)) / n1
    c1, s1 = np.cos(a1), np.sin(a1)
    m1 = np.block([[c1, -s1], [-s1, -c1]])
    at = 2.0 * np.pi * np.outer(np.arange(n2), np.arange(n1)) / s
    ct = np.cos(at)[:, :, None]
    st = np.sin(at)[:, :, None]
    a2 = 2.0 * np.pi * np.outer(np.arange(n2), np.arange(n2)) / n2
    scale = 1.0 / math.sqrt(float(s) * m)
    jb = 16
    gmat = np.zeros((n2, jb, 2, n2, jb), dtype=np.float64)
    for j in range(jb):
        gmat[:, j, 0, :, j] = np.cos(a2) * scale
        gmat[:, j, 1, :, j] = np.sin(a2) * scale
    gmat = gmat.reshape(n2 * jb, 2 * n2 * jb)
    return (jnp.asarray(cs, BF16), jnp.asarray(m1, BF16), jnp.asarray(ct, F32), jnp.asarray(st, F32),
            jnp.asarray(gmat, BF16))


def _fnet1_body(x_ref, g_ref, sc_ref, sh_ref, cs_ref, m1_ref, ct_ref, st_ref, o_ref, pq_scr, *, nb):
    n1 = DFT_N1
    g = g_ref[...]
    sp = 1.0 + sc_ref[...]
    sh = sh_ref[...]
    cs = cs_ref[...]
    m1 = m1_ref[...]
    gd = FNET_GROUP_DIM
    for j in range(nb):
        h = _norm_mod_rows(x_ref[:, j, :], g, sp, sh).astype(BF16)
        for gi in range(FNET_GROUPS):
            cols = slice(gi * gd, (gi + 1) * gd)
            pq = jnp.dot(h[:, cols], cs, preferred_element_type=F32)
            pq_scr[0:n1, cols] = pq[:, :gd].astype(BF16)
            pq_scr[n1:2 * n1, cols] = pq[:, gd:].astype(BF16)
        a = jnp.dot(m1, pq_scr[...], preferred_element_type=F32)
        ar = a[:n1]
        ai = a[n1:]
        ct = ct_ref[j]
        st = st_ref[j]
        o_ref[0, j] = (ar * ct + ai * st).astype(BF16)
        o_ref[1, j] = (ai * ct - ar * st).astype(BF16)


def _fnet_stage1(x, g, sc, sh, cs, m1, ct, st, nb=8):
    s, d = x.shape
    n1 = DFT_N1
    n2 = s // n1
    body = functools.partial(_fnet1_body, nb=nb)
    row = _full((1, d))
    return pl.pallas_call(
        body,
        out_shape=jax.ShapeDtypeStruct((2, n2, n1, d), BF16),
        grid=(n2 // nb,),
        in_specs=[
            pl.BlockSpec((n1, nb, d), lambda b: (0, b, 0)),
            row, row, row,
            _full(cs.shape),
            _full(m1.shape),
            pl.BlockSpec((nb, n1, 1), lambda b: (b, 0, 0)),
            pl.BlockSpec((nb, n1, 1), lambda b: (b, 0, 0)),
        ],
        out_specs=pl.BlockSpec((2, nb, n1, d), lambda b: (0, b, 0, 0)),
        scratch_shapes=[pltpu.VMEM((2 * n1, d), BF16)],
        compiler_params=_cparams(("arbitrary",), 48),
        name="fnet_stage1",
    )(x.reshape(n1, n2, d), g, sc, sh, cs, m1, ct, st)


def _fnet2_body(a_ref, g_ref, o_ref):
    blk = a_ref[...]
    rows = blk.shape[0] * blk.shape[1] * blk.shape[2]
    b2 = blk.reshape(rows, blk.shape[3])
    f = jnp.dot(g_ref[...], b2, preferred_element_type=F32)
    o_ref[...] = f.astype(o_ref.dtype).reshape(o_ref.shape)


def _fnet_stage2(a4, gmat, jb=16):
    _, n2, n1, d = a4.shape
    return pl.pallas_call(
        _fnet2_body,
        out_shape=jax.ShapeDtypeStruct((n2, n1, d), BF16),
        grid=(n1 // jb,),
        in_specs=[
            pl.BlockSpec((2, n2, jb, d), lambda b: (0, 0, b, 0)),
            _full(gmat.shape),
        ],
        out_specs=pl.BlockSpec((n2, jb, d), lambda b: (0, b, 0)),
        compiler_params=_cparams(("arbitrary",), 48),
        name="fnet_stage2",
    )(a4, gmat)


def _pool_body(xp_ref, x_ref, xn_ref, g_ref, sc_ref, sh_ref, w_ref, ps_ref, g1_ref, o_ref, hbuf, w_scr, *, ts, seq):
    i = pl.program_id(0)
    ni = pl.num_programs(0)
    halo = 8

    @pl.when(i == 0)
    def _():
        w_scr[...] = w_ref[...].astype(BF16)

    g = g_ref[...]
    sp = 1.0 + sc_ref[...]
    sh = sh_ref[...]
    hbuf[0:halo, :] = jnp.where(i > 0, _norm_mod_rows(xp_ref[...], g, sp, sh), 0.0)
    _norm_mod_store(x_ref, g_ref, sc_ref, sh_ref, hbuf.at[pl.ds(halo, ts), :], ts)
    hbuf[halo + ts:2 * halo + ts, :] = jnp.where(i < ni - 1, _norm_mod_rows(xn_ref[...], g, sp, sh), 0.0)

    t = i * ts + lax.broadcasted_iota(I32, (ts, 1), 0)
    gd = POOL_GROUP_DIM
    for gi, win in enumerate(POOL_WINDOWS):
        cols = slice(gi * gd, (gi + 1) * gd)
        half = win // 2
        wsum = hbuf[halo - half:halo - half + ts, cols]
        for off in range(-half + 1, win - half):
            wsum = wsum + hbuf[halo + off:halo + off + ts, cols]
        count = (jnp.minimum(t + (win - half), seq) - jnp.maximum(t - half, 0)).astype(F32)
        mixed = wsum / count - hbuf[halo:halo + ts, cols]
        y = jnp.dot(mixed.astype(BF16), w_scr[gi], preferred_element_type=F32)
        o_ref[:, cols] = x_ref[:, cols] + g1_ref[:, cols] * (y * ps_ref[:, cols])


def _pool_layer(x, g, sc, sh, w_pool, pscale, g1, ts=512):
    s, d = x.shape
    hb = ts // 8
    nhb = s // 8
    body = functools.partial(_pool_body, ts=ts, seq=s)
    row = _full((1, d))
    return pl.pallas_call(
        body,
        out_shape=jax.ShapeDtypeStruct((s, d), F32),
        grid=(s // ts,),
        in_specs=[
            pl.BlockSpec((8, d), lambda i: (jnp.maximum(i * hb - 1, 0), 0)),
            pl.BlockSpec((ts, d), lambda i: (i, 0)),
            pl.BlockSpec((8, d), lambda i: (jnp.minimum((i + 1) * hb, nhb - 1), 0)),
            row, row, row,
            _full(w_pool.shape),
            row, row,
        ],
        out_specs=pl.BlockSpec((ts, d), lambda i: (i, 0)),
        scratch_shapes=[pltpu.VMEM((ts + 16, d), F32), pltpu.VMEM(w_pool.shape, BF16)],
        compiler_params=_cparams(("arbitrary",), 48),
        name="pool_layer",
    )(x, x, x, g, sc, sh, w_pool, pscale, g1)


SLAB_X = D_MODEL // LANES
SLAB_ROWS = SLAB_X + 8


def _router_body(x_ref, g_ref, sc_ref, sh_ref, rw_ref, rb_ref, grp_ref, slab_ref, h_scr):
    tm = h_scr.shape[0]
    _norm_mod_store(x_ref, g_ref, sc_ref, sh_ref, h_scr, tm)
    h = h_scr[...]
    h_hi = h.astype(BF16)
    h_lo = (h - h_hi.astype(F32)).astype(BF16)
    rw = rw_ref[...]
    rw_hi = rw.astype(BF16)
    rw_lo = (rw - rw_hi.astype(F32)).astype(BF16)
    nt = (((1,), (1,)), ((), ()))
    p_hi = lax.dot_general(jnp.concatenate([rw_hi, rw_lo], axis=0), h_hi, nt, preferred_element_type=F32)
    p_lo = lax.dot_general(rw_hi, h_lo, nt, preferred_element_type=F32)
    logits = p_hi[:N_EXPERTS] + (p_hi[N_EXPERTS:] + p_lo)
    sc = jax.nn.sigmoid(logits)
    sel = sc + rb_ref[...]
    epg = EXPERTS_PER_GROUP
    rows = [sel[e:e + 1, :] for e in range(N_EXPERTS)]
    srow = [sc[e:e + 1, :] for e in range(N_EXPERTS)]
    gscore = []
    for gi in range(N_EXPERT_GROUPS):
        r = rows[gi * epg:(gi + 1) * epg]
        best = None
        for a in range(epg):
            for b in range(a + 1, epg):
                ps = r[a] + r[b]
                best = ps if best is None else jnp.maximum(best, ps)
        gscore.append(best)
    gidx = jnp.zeros_like(gscore[0], dtype=I32)
    gbest = gscore[0]
    for gi in range(1, N_EXPERT_GROUPS):
        better = gscore[gi] > gbest
        gidx = jnp.where(better, gi, gidx)
        gbest = jnp.where(better, gscore[gi], gbest)
    vin = []
    sin_ = []
    for j in range(epg):
        v = rows[j]
        sv = srow[j]
        for gi in range(1, N_EXPERT_GROUPS):
            pick = gidx == gi
            v = jnp.where(pick, rows[gi * epg + j], v)
            sv = jnp.where(pick, srow[gi * epg + j], sv)
        vin.append(v)
        sin_.append(sv)
    l1 = jnp.zeros_like(gidx)
    b1 = vin[0]
    for j in range(1, epg):
        better = vin[j] > b1
        l1 = jnp.where(better, j, l1)
        b1 = jnp.where(better, vin[j], b1)
    neg = jnp.full_like(b1, -jnp.inf)
    l2 = jnp.zeros_like(gidx)
    b2 = neg
    for j in range(epg):
        vj = jnp.where(l1 == j, neg, vin[j])
        better = vj > b2
        l2 = jnp.where(better, j, l2)
        b2 = jnp.where(better, vj, b2)
    w1 = sin_[0]
    w2 = sin_[0]
    for j in range(1, epg):
        w1 = jnp.where(l1 == j, sin_[j], w1)
        w2 = jnp.where(l2 == j, sin_[j], w2)
    tot = w1 + w2
    w1 = w1 / tot
    w2 = w2 / tot
    grp_ref[...] = gidx
    zero = jnp.zeros_like(w1)
    rid = lax.broadcasted_iota(I32, (LANES, tm), 0)
    info = jnp.zeros((LANES, tm), F32)
    for j in range(epg):
        cwj = jnp.where(l1 == j, w1, zero) + jnp.where(l2 == j, w2, zero)
        info = jnp.where(rid == j, cwj, info)
    slab_ref[:, 0:SLAB_X, :] = x_ref[...].reshape(tm, SLAB_X, LANES)
    tail = jnp.concatenate([info.T, jnp.zeros((tm, (SLAB_ROWS - SLAB_X - 1) * LANES), F32)], axis=1)
    slab_ref[:, SLAB_X:SLAB_ROWS, :] = tail.reshape(tm, SLAB_ROWS - SLAB_X, LANES)


def _router(x, g, sc, sh, rw_t, rb_col, tm=512):
    s, d = x.shape
    row = _full((1, d))
    return pl.pallas_call(
        _router_body,
        out_shape=(jax.ShapeDtypeStruct((1, s), I32), jax.ShapeDtypeStruct((s, SLAB_ROWS, LANES), F32)),
        grid=(s // tm,),
        in_specs=[
            pl.BlockSpec((tm, d), lambda i: (i, 0)),
            row, row, row,
            _full((N_EXPERTS, d)),
            _full((N_EXPERTS, 1)),
        ],
        out_specs=(pl.BlockSpec((1, tm), lambda i: (0, i)),
                   pl.BlockSpec((tm, SLAB_ROWS, LANES), lambda i: (i, 0, 0))),
        scratch_shapes=[pltpu.VMEM((tm, d), F32)],
        compiler_params=_cparams(("arbitrary",), 40),
        name="router",
    )(x, g, sc, sh, rw_t, rb_col)


def _invert_body(dest_ref, src_ref):
    n_slots = src_ref.shape[0]
    n_tok = dest_ref.shape[0]

    def clear(i, carry):
        src_ref[i] = 0
        return carry

    lax.fori_loop(0, n_slots, clear, 0, unroll=8)

    def fill(t, carry):
        src_ref[dest_ref[t]] = t
        return carry

    lax.fori_loop(0, n_tok, fill, 0, unroll=8)


def _invert(dest, n_slots):
    return pl.pallas_call(
        _invert_body,
        out_shape=jax.ShapeDtypeStruct((n_slots,), I32),
        grid_spec=pltpu.PrefetchScalarGridSpec(
            num_scalar_prefetch=1,
            grid=(1,),
            in_specs=[],
            out_specs=pl.BlockSpec(memory_space=pltpu.SMEM),
        ),
        compiler_params=_cparams(("arbitrary",), 16),
        name="moe_invert",
    )(dest)


def _slab_to_rows_body(x3_ref, o_ref):
    o_ref[...] = x3_ref[...].reshape(o_ref.shape)


def _slab_to_rows(x3, s, tm=512):
    d = SLAB_X * LANES
    return pl.pallas_call(
        _slab_to_rows_body,
        out_shape=jax.ShapeDtypeStruct((s, d), F32),
        grid=(s // tm,),
        in_specs=[pl.BlockSpec((tm, SLAB_X, LANES), lambda i: (i, 0, 0))],
        out_specs=pl.BlockSpec((tm, d), lambda i: (i, 0)),
        compiler_params=_cparams(("arbitrary",), 32),
        name="slab_to_rows",
    )(x3)


def _moe_body(gid_ref, nu_ref, nv_ref, src_ref, slab_hbm, g_ref, sc_ref, sh_ref, g2_ref, wg_ref, wu_ref, wd_ref,
              out_hbm, xbuf, obuf, h_scr, acc_scr, act_scr, gsem, ssem, *, nslot):
    c = pl.program_id(0)
    sl = pl.program_id(1)
    nused = nu_ref[0]
    used = c < nused
    rows = h_scr.shape[0]
    per = rows // nslot
    buf = c % 2

    def gather_copy(chunk, r, b):
        tok = src_ref[chunk * rows + r]
        return pltpu.make_async_copy(slab_hbm.at[pl.ds(tok, 1)], xbuf.at[b, pl.ds(r, 1)], gsem.at[b])

    def start_gather(chunk, part, b):
        def body(i, carry):
            gather_copy(chunk, part * per + i, b).start()
            return carry

        lax.fori_loop(0, per, body, 0, unroll=8)

    n_tok = out_hbm.shape[0] - rows

    def scatter_copy(chunk, r):
        tok = jnp.where(r < nv_ref[chunk], src_ref[chunk * rows + r], n_tok + r)
        return pltpu.make_async_copy(obuf.at[pl.ds(r, 1)], out_hbm.at[pl.ds(tok, 1)], ssem.at[0])

    def wait_scatter():
        pltpu.make_async_copy(obuf, out_hbm.at[pl.ds(0, rows)], ssem.at[0]).wait()

    @pl.when(jnp.logical_and(c == 0, sl == 0))
    def _():
        for part in range(nslot):
            start_gather(0, part, 0)
        obuf[...] = jnp.zeros(obuf.shape, F32)
        init = pltpu.make_async_copy(obuf, out_hbm.at[pl.ds(n_tok, rows)], ssem.at[0])
        init.start()
        init.wait()

    @pl.when(jnp.logical_and(used, sl == 0))
    def _():
        pltpu.make_async_copy(slab_hbm.at[pl.ds(0, rows)], xbuf.at[buf], gsem.at[buf]).wait()

    @pl.when(c + 1 < nused)
    def _():
        start_gather(c + 1, sl, 1 - buf)

    @pl.when(jnp.logical_and(used, sl == 0))
    def _():
        g = g_ref[...]
        sp = 1.0 + sc_ref[...]
        sh = sh_ref[...]
        chunk = 32

        def body(k, carry):
            r = pl.multiple_of(k * chunk, chunk)
            x = xbuf[buf, pl.ds(r, chunk), 0:SLAB_X, :].reshape(chunk, SLAB_X * LANES)
            h_scr[pl.ds(r, chunk), :] = _norm_mod_rows(x, g, sp, sh).astype(BF16)
            return carry

        lax.fori_loop(0, rows // chunk, body, 0)
        acc_scr[...] = jnp.zeros(acc_scr.shape, F32)

    @pl.when(used)
    def _():
        h = h_scr[...]
        gate = jnp.dot(h, wg_ref[0, 0].astype(BF16), preferred_element_type=F32)
        up = jnp.dot(h, wu_ref[0, 0].astype(BF16), preferred_element_type=F32)
        j = sl // MOE_FF_SPLIT
        info = xbuf[buf, :, SLAB_X:SLAB_ROWS, :].reshape(rows, (SLAB_ROWS - SLAB_X) * LANES)[:, :LANES]
        cw = jnp.zeros((rows, 1), F32)
        for e in range(EXPERTS_PER_GROUP):
            cw = jnp.where(j == e, info[:, e:e + 1], cw)
        act = (gate * jax.nn.sigmoid(gate)) * up * cw
        part = sl % MOE_FF_SPLIT
        act_scr[part] = act.astype(BF16)

        @pl.when(part == MOE_FF_SPLIT - 1)
        def _():
            full = jnp.concatenate([act_scr[k] for k in range(MOE_FF_SPLIT)], axis=1)
            acc_scr[...] += jnp.dot(full, wd_ref[0, 0].astype(BF16), preferred_element_type=F32)

    @pl.when(jnp.logical_and(used, sl == nslot - 1))
    def _():
        @pl.when(c > 0)
        def _():
            wait_scatter()

        y = g2_ref[...] * acc_scr[...]
        obuf[...] = xbuf[buf, :, 0:SLAB_X, :] + y.reshape(rows, SLAB_X, LANES)

        def sbody(r, carry):
            scatter_copy(c, r).start()
            return carry

        lax.fori_loop(0, rows, sbody, 0, unroll=8)

        @pl.when(c == nused - 1)
        def _():
            wait_scatter()


def _moe_experts(gid, nused, nvalid, src, slabs, g, sc, sh, g2, w_gate, w_up, w_down, layer):
    s = slabs.shape[0]
    d = SLAB_X * LANES
    cm = MOE_CHUNK
    nc = src.shape[0] // cm
    nslot = EXPERTS_PER_GROUP * MOE_FF_SPLIT
    fs = EXPERT_FF // MOE_FF_SPLIT

    def eidx(c, sl, gid_ref, nu_ref):
        last = nu_ref[0] - 1
        cc = jnp.minimum(c, last)
        used = c <= last
        e = EXPERTS_PER_GROUP * gid_ref[cc] + jnp.where(used, sl // MOE_FF_SPLIT, EXPERTS_PER_GROUP - 1)
        half = jnp.where(used, sl % MOE_FF_SPLIT, MOE_FF_SPLIT - 1)
        return e, half

    def wgmap(c, sl, gid_ref, nu_ref, nv_ref, src_ref):
        e, half = eidx(c, sl, gid_ref, nu_ref)
        return (layer, e, 0, half)

    def wdmap(c, sl, gid_ref, nu_ref, nv_ref, src_ref):
        e, _ = eidx(c, sl, gid_ref, nu_ref)
        return (layer, e, 0, 0)

    row = pl.BlockSpec((1, d), lambda c, sl, *_: (0, 0))
    anyspec = pl.BlockSpec(memory_space=pl.ANY)
    body = functools.partial(_moe_body, nslot=nslot)
    return pl.pallas_call(
        body,
        out_shape=jax.ShapeDtypeStruct((s + cm, SLAB_X, LANES), F32),
        grid_spec=pltpu.PrefetchScalarGridSpec(
            num_scalar_prefetch=4,
            grid=(nc, nslot),
            in_specs=[
                anyspec,
                row, row, row, row,
                pl.BlockSpec((1, 1, d, fs), wgmap),
                pl.BlockSpec((1, 1, d, fs), wgmap),
                pl.BlockSpec((1, 1, EXPERT_FF, d), wdmap),
            ],
            out_specs=anyspec,
            scratch_shapes=[
                pltpu.VMEM((2, cm, SLAB_ROWS, LANES), F32),
                pltpu.VMEM((cm, SLAB_X, LANES), F32),
                pltpu.VMEM((cm, d), BF16),
                pltpu.VMEM((cm, d), F32),
                pltpu.VMEM((MOE_FF_SPLIT, cm, fs), BF16),
                pltpu.SemaphoreType.DMA((2,)),
                pltpu.SemaphoreType.DMA((1,)),
            ],
        ),
        compiler_params=_cparams(("arbitrary", "arbitrary"), 56),
        name="moe_experts",
    )(gid, nused, nvalid, src, slabs, g, sc, sh, g2, w_gate, w_up, w_down)


def _dispatch_tables(grp, nc):
    cm = MOE_CHUNK
    ng = N_EXPERT_GROUPS
    oh = (grp[:, None] == jnp.arange(ng, dtype=I32)[None, :]).astype(I32)
    cs = jnp.cumsum(oh, axis=0)
    counts = cs[-1]
    rank = jnp.sum(cs * oh, axis=1) - 1
    nch = (counts + cm - 1) // cm
    cum = jnp.cumsum(nch)
    first = cum - nch
    dest = jnp.sum(oh * (first * cm)[None, :], axis=1) + rank
    nused = cum[-1:]
    cidx = jnp.arange(nc, dtype=I32)
    gid = jnp.minimum(jnp.sum((cidx[:, None] >= cum[None, :]).astype(I32), axis=1), ng - 1)
    goh = (gid[:, None] == jnp.arange(ng, dtype=I32)[None, :]).astype(I32)
    left = jnp.sum(goh * counts[None, :], axis=1) - (cidx - jnp.sum(goh * first[None, :], axis=1)) * cm
    nvalid = jnp.where(cidx < cum[-1], jnp.clip(left, 0, cm), 0)
    return dest.astype(I32), gid.astype(I32), nused.astype(I32), nvalid.astype(I32)


def _moe_layer(x, g, sc, sh, g2, rw_t, rb_col, w_gate, w_up, w_down, layer):
    s, d = x.shape
    nc = s // MOE_CHUNK + N_EXPERT_GROUPS
    grp, slabs = _router(x, g, sc, sh, rw_t, rb_col)
    dest, gid, nused, nvalid = _dispatch_tables(grp[0], nc)
    src = _invert(dest, nc * MOE_CHUNK)
    y3 = _moe_experts(gid, nused, nvalid, src, slabs, g, sc, sh, g2, w_gate, w_up, w_down, layer)
    return _slab_to_rows(y3, s)


def _final_body(x_ref, g_ref, o_ref):
    x = x_ref[...]
    ms = jnp.mean(x * x, axis=-1, keepdims=True)
    o_ref[...] = (x * lax.rsqrt(ms + EPS)) * g_ref[...]


def _final_norm(x, g, tm=256):
    s, d = x.shape
    return pl.pallas_call(
        _final_body,
        out_shape=jax.ShapeDtypeStruct((s, d), F32),
        grid=(s // tm,),
        in_specs=[pl.BlockSpec((tm, d), lambda i: (i, 0)), _full((1, d))],
        out_specs=pl.BlockSpec((tm, d), lambda i: (i, 0)),
        compiler_params=_cparams(("arbitrary",), 32),
        name="final_norm",
    )(x, g)


def _swap_halves(w):
    half = w.shape[-1] // 2
    return jnp.concatenate([w[..., half:], w[..., :half]], axis=-1)


def kernel(x, c, positions, ada_w, ada_b, norm_g, final_g, conv_w1, conv_b1, conv_wdw, conv_bdw, conv_ln_g, conv_ln_b, conv_w2, conv_b2, mla_w_down, mla_g_q, mla_g_kv, mla_w_uq, mla_w_ukv, mla_w_o, fnet_w, fnet_b, pool_w, pool_scale, router_w, router_b, moe_w_gate, moe_w_up, moe_w_down):
    b, s, d = x.shape
    assert b == 1 and d == D_MODEL
    xs = x.reshape(s, d)
    mod = _ada_mod(c, ada_w, ada_b).reshape(DEPTH, 6, 1, d)
    rw_t = router_w.T
    rb_col = router_b.reshape(N_EXPERTS, 1)
    zero_bias = jnp.zeros((1, d), F32)

    for i in range(DEPTH):
        sh1, sc1, g1, sh2, sc2, g2 = [mod[i, k] for k in range(6)]
        ng1 = norm_g[i, 0].reshape(1, d)
        ng2 = norm_g[i, 1].reshape(1, d)
        kind, j = i % 4, i // 4
        if kind == 0:
            u = _conv_glu(xs, ng1, sc1, sh1, conv_w1[j], conv_b1[j])
            dd = _dwconv_ln_silu(u, conv_wdw[j].reshape(CONV_WIDTH, d), conv_bdw[j].reshape(1, d),
                                 conv_ln_g[j].reshape(1, d), conv_ln_b[j].reshape(1, d))
            xs = _mm_res(dd, conv_w2[j], conv_b2[j].reshape(1, d), xs, g1)
        elif kind == 1:
            wd = mla_w_down[j]
            w_ext = jnp.concatenate([wd, _swap_halves(wd[:, Q_LORA + KV_LORA:])], axis=1)
            inv = ROPE_THETA ** (-jnp.arange(0, QK_ROPE_DIM, 2, dtype=F32) / QK_ROPE_DIM)
            zeros64 = jnp.zeros((QK_ROPE_DIM,), F32)
            half = QK_ROPE_DIM // 2
            rope_c = jnp.zeros((8, LANES), F32)
            rope_c = rope_c.at[0].set(jnp.concatenate([inv, inv, zeros64]))
            rope_c = rope_c.at[1].set(jnp.concatenate([jnp.ones((QK_ROPE_DIM,), F32), zeros64]))
            rope_c = rope_c.at[2].set(jnp.concatenate([-jnp.ones((half,), F32), jnp.ones((half,), F32), zeros64]))
            cq, ckv, kpe, cz, sz = _mla_down(xs, ng1, sc1, sh1, w_ext, mla_g_q[j].reshape(1, Q_LORA),
                                             mla_g_kv[j].reshape(1, KV_LORA), positions.reshape(s, 1), rope_c)
            wq = mla_w_uq[j]
            wq_ext = jnp.concatenate([wq, _swap_halves(wq[..., QK_NOPE_DIM:])], axis=-1)
            wq_ext = jnp.transpose(wq_ext, (1, 0, 2))
            wkv2d = mla_w_ukv[j].reshape(KV_LORA, N_HEADS * (QK_NOPE_DIM + V_HEAD_DIM))
            qh, kh, vh = _mla_up(cq, ckv, kpe, cz, sz, wq_ext, wkv2d)
            o = _attention(qh, kh, vh)
            xs = _mm_res(o, mla_w_o[j].reshape(N_HEADS * V_HEAD_DIM, d), zero_bias, xs, g1)
        elif kind == 2:
            cs, m1, ct, st, gmat = _dft_tables(s)
            a4 = _fnet_stage1(xs, ng1, sc1, sh1, cs, m1, ct, st)
            f = _fnet_stage2(a4, gmat)
            xs = _mm_res(f.reshape(s, d), fnet_w[j], fnet_b[j].reshape(1, d), xs, g1)
        else:
            xs = _pool_layer(xs, ng1, sc1, sh1, pool_w[j], pool_scale[j].reshape(1, d), g1)
        xs = _moe_layer(xs, ng2, sc2, sh2, g2, rw_t, rb_col, moe_w_gate, moe_w_up, moe_w_down, i)

    return _final_norm(xs, final_g.reshape(1, d)).reshape(b, s, d)
```

```python
import functools
import math

import numpy as np
import jax
import jax.numpy as jnp
from jax import lax
from jax.experimental import pallas as pl
from jax.experimental.pallas import tpu as pltpu

F32 = jnp.float32
BF16 = jnp.bfloat16
I32 = jnp.int32

D_MODEL = 2048
DEPTH = 4
EPS = 1e-6
CONV_WIDTH = 31
CONV_PAD = CONV_WIDTH // 2
N_HEADS = 16
Q_LORA = 512
KV_LORA = 512
QK_NOPE_DIM = 128
QK_ROPE_DIM = 64
V_HEAD_DIM = 128
QK_HEAD_DIM = QK_NOPE_DIM + QK_ROPE_DIM
ROPE_THETA = 10000.0
FNET_GROUPS = 4
FNET_GROUP_DIM = D_MODEL // FNET_GROUPS
POOL_WINDOWS = (2, 4, 8, 16)
POOL_GROUP_DIM = D_MODEL // len(POOL_WINDOWS)
N_EXPERTS = 16
N_EXPERT_GROUPS = 4
EXPERTS_PER_GROUP = 4
EXPERT_FF = 512

LANES = 128
HEAD_PAD = 256
DFT_N1 = 128
MOE_CHUNK = 512
MOE_FF_SPLIT = 1
MIB = 1 << 20


def _cparams(sem, vmem_mib):
    return pltpu.CompilerParams(dimension_semantics=sem, vmem_limit_bytes=vmem_mib * MIB)


def _full(shape):
    nd = len(shape)
    return pl.BlockSpec(shape, lambda *_: (0,) * nd)


def _norm_mod_rows(x, g, sp, sh):
    ms = jnp.mean(x * x, axis=-1, keepdims=True)
    y = x * lax.rsqrt(ms + EPS)
    return (y * g) * sp + sh


def _norm_mod_store(x_ref, g_ref, sc_ref, sh_ref, out_ref, rows, chunk=32):
    g = g_ref[...]
    sp = 1.0 + sc_ref[...]
    sh = sh_ref[...]

    def body(c, carry):
        r = pl.multiple_of(c * chunk, chunk)
        x = x_ref[pl.ds(r, chunk), :]
        out_ref[pl.ds(r, chunk), :] = _norm_mod_rows(x, g, sp, sh).astype(out_ref.dtype)
        return carry

    lax.fori_loop(0, rows // chunk, body, 0)


def _mod_body(c_ref, w_ref, b_ref, o_ref):
    k_dim = c_ref.shape[0]
    tn = o_ref.shape[-1]
    rc = 16

    def body(k, acc):
        r = pl.multiple_of(k * rc, rc)
        c = c_ref[pl.ds(r, rc), :]
        ca = c * jax.nn.sigmoid(c)
        return acc + w_ref[0, pl.ds(r, rc), :] * ca

    acc = lax.fori_loop(0, k_dim // rc, body, jnp.zeros((rc, tn), F32), unroll=4)
    o_ref[0] = jnp.sum(acc, axis=0, keepdims=True) + b_ref[0]


def _ada_mod(c, ada_w, ada_b):
    depth, d, n = ada_w.shape
    tn = 1024
    return pl.pallas_call(
        _mod_body,
        out_shape=jax.ShapeDtypeStruct((depth, 1, n), F32),
        grid=(depth, n // tn),
        in_specs=[
            _full((d, 1)),
            pl.BlockSpec((1, d, tn), lambda i, j: (i, 0, j)),
            pl.BlockSpec((1, 1, tn), lambda i, j: (i, 0, j)),
        ],
        out_specs=pl.BlockSpec((1, 1, tn), lambda i, j: (i, 0, j)),
        compiler_params=_cparams(("arbitrary", "arbitrary"), 40),
        name="ada_mod",
    )(c.reshape(d, 1), ada_w, ada_b.reshape(depth, 1, n))


def _mm_res_body(a_ref, w_ref, b_ref, x_ref, g_ref, o_ref):
    y = jnp.dot(a_ref[...], w_ref[...].astype(BF16), preferred_element_type=F32) + b_ref[...]
    o_ref[...] = x_ref[...] + g_ref[...] * y


def _mm_res(a, w, b, x, gate, tm=1024, tn=512):
    m, k = a.shape
    n = w.shape[1]
    return pl.pallas_call(
        _mm_res_body,
        out_shape=jax.ShapeDtypeStruct((m, n), F32),
        grid=(m // tm, n // tn),
        in_specs=[
            pl.BlockSpec((tm, k), lambda i, j: (i, 0)),
            pl.BlockSpec((k, tn), lambda i, j: (0, j)),
            pl.BlockSpec((1, tn), lambda i, j: (0, j)),
            pl.BlockSpec((tm, tn), lambda i, j: (i, j)),
            pl.BlockSpec((1, tn), lambda i, j: (0, j)),
        ],
        out_specs=pl.BlockSpec((tm, tn), lambda i, j: (i, j)),
        compiler_params=_cparams(("arbitrary", "arbitrary"), 48),
        name="mm_res",
    )(a, w, b, x, gate)


def _conv1_body(x_ref, g_ref, sc_ref, sh_ref, wa_ref, wb_ref, ba_ref, bb_ref, u_ref, h_scr):
    @pl.when(pl.program_id(1) == 0)
    def _():
        _norm_mod_store(x_ref, g_ref, sc_ref, sh_ref, h_scr, h_scr.shape[0])

    h = h_scr[...]
    a = jnp.dot(h, wa_ref[...].astype(BF16), preferred_element_type=F32) + ba_ref[...]
    b = jnp.dot(h, wb_ref[...].astype(BF16), preferred_element_type=F32) + bb_ref[...]
    u_ref[...] = a * jax.nn.sigmoid(b)


def _conv_glu(x, g, sc, sh, w1, b1, tm=1024, tn=512):
    s, d = x.shape
    nb = d // tn
    b1r = b1.reshape(1, 2 * d)
    row = _full((1, d))
    return pl.pallas_call(
        _conv1_body,
        out_shape=jax.ShapeDtypeStruct((s, d), F32),
        grid=(s // tm, nb),
        in_specs=[
            pl.BlockSpec((tm, d), lambda i, j: (i, 0)),
            row, row, row,
            pl.BlockSpec((d, tn), lambda i, j: (0, j)),
            pl.BlockSpec((d, tn), lambda i, j: (0, j + nb)),
            pl.BlockSpec((1, tn), lambda i, j: (0, j)),
            pl.BlockSpec((1, tn), lambda i, j: (0, j + nb)),
        ],
        out_specs=pl.BlockSpec((tm, tn), lambda i, j: (i, j)),
        scratch_shapes=[pltpu.VMEM((tm, d), BF16)],
        compiler_params=_cparams(("arbitrary", "arbitrary"), 56),
        name="conv_glu",
    )(x, g, sc, sh, w1, w1, b1r, b1r)


def _dwconv_body(up_ref, u_ref, un_ref, w_ref, bdw_ref, lg_ref, lb_ref, d_ref, buf, cv, *, ts, cw, ncw):
    i = pl.program_id(0)
    cj = pl.program_id(1)
    ni = pl.num_programs(0)
    halo = 16
    buf[0:halo, :] = jnp.where(i > 0, up_ref[...], 0.0)
    buf[halo:halo + ts, :] = u_ref[...]
    buf[halo + ts:2 * halo + ts, :] = jnp.where(i < ni - 1, un_ref[...], 0.0)
    rc = 32
    for lc in range(cw // LANES):
        ls = slice(lc * LANES, (lc + 1) * LANES)
        wcol = w_ref[:, ls]
        bcol = bdw_ref[:, ls]

        def body(r, carry, ls=ls, wcol=wcol, bcol=bcol):
            r0 = pl.multiple_of(r * rc, rc)
            wrows = rc + 2 * halo
            win = buf[pl.ds(r0, wrows), ls]
            acc = jnp.zeros((rc, LANES), F32)
            for phase in range(8):
                shifted = win if phase == 0 else pltpu.roll(win, wrows - phase, 0)
                for k in range(CONV_WIDTH):
                    off = halo - CONV_PAD + k
                    if off % 8 == phase:
                        acc = acc + shifted[off - phase:off - phase + rc, :] * wcol[k:k + 1, :]
            cv[cj, pl.ds(r0, rc), ls] = acc + bcol
            return carry

        lax.fori_loop(0, ts // rc, body, 0)

    @pl.when(cj == ncw - 1)
    def _():
        d_model = ncw * cw
        rc2 = 32

        def body2(r, carry):
            r0 = pl.multiple_of(r * rc2, rc2)
            parts = [cv[c, pl.ds(r0, rc2), :] for c in range(ncw)]
            tot = parts[0].sum(axis=-1, keepdims=True)
            for p in parts[1:]:
                tot = tot + p.sum(axis=-1, keepdims=True)
            mu = tot / d_model
            cen = [p - mu for p in parts]
            sq = (cen[0] * cen[0]).sum(axis=-1, keepdims=True)
            for p in cen[1:]:
                sq = sq + (p * p).sum(axis=-1, keepdims=True)
            rinv = lax.rsqrt(sq / d_model + EPS)
            for c in range(ncw):
                cs = slice(c * cw, (c + 1) * cw)
                y = (cen[c] * rinv) * lg_ref[:, cs] + lb_ref[:, cs]
                d_ref[pl.ds(r0, rc2), cs] = (y * jax.nn.sigmoid(y)).astype(d_ref.dtype)
            return carry

        lax.fori_loop(0, ts // rc2, body2, 0)


def _dwconv_ln_silu(u, wdw, bdw, ln_g, ln_b, ts=512, cw=512):
    s, d = u.shape
    ncw = d // cw
    hb = ts // 16
    nhb = s // 16
    body = functools.partial(_dwconv_body, ts=ts, cw=cw, ncw=ncw)
    return pl.pallas_call(
        body,
        out_shape=jax.ShapeDtypeStruct((s, d), BF16),
        grid=(s // ts, ncw),
        in_specs=[
            pl.BlockSpec((16, cw), lambda i, j: (jnp.maximum(i * hb - 1, 0), j)),
            pl.BlockSpec((ts, cw), lambda i, j: (i, j)),
            pl.BlockSpec((16, cw), lambda i, j: (jnp.minimum((i + 1) * hb, nhb - 1), j)),
            pl.BlockSpec((CONV_WIDTH, cw), lambda i, j: (0, j)),
            pl.BlockSpec((1, cw), lambda i, j: (0, j)),
            _full((1, d)),
            _full((1, d)),
        ],
        out_specs=pl.BlockSpec((ts, d), lambda i, j: (i, 0)),
        scratch_shapes=[pltpu.VMEM((ts + 32, cw), F32), pltpu.VMEM((ncw, ts, cw), F32)],
        compiler_params=_cparams(("arbitrary", "arbitrary"), 32),
        name="dwconv_ln_silu",
    )(u, u, u, wdw, bdw, ln_g, ln_b)


def _rope_rot(t, cz, sz):
    return t * cz + pltpu.roll(t, QK_ROPE_DIM, 1) * sz


def _mla_down_body(x_ref, g_ref, sc_ref, sh_ref, w_ref, gq_ref, gkv_ref, pos_ref, rc_ref,
                   cq_ref, ckv_ref, kpe_ref, cz_ref, sz_ref, h_scr, w_scr):
    @pl.when(pl.program_id(0) == 0)
    def _():
        w_scr[...] = w_ref[...].astype(BF16)

    _norm_mod_store(x_ref, g_ref, sc_ref, sh_ref, h_scr, h_scr.shape[0])
    down = jnp.dot(h_scr[...], w_scr[...], preferred_element_type=F32)
    cq = down[:, :Q_LORA]
    ckv = down[:, Q_LORA:Q_LORA + KV_LORA]
    cq_ref[...] = (cq * lax.rsqrt(jnp.mean(cq * cq, axis=-1, keepdims=True) + EPS) * gq_ref[...]).astype(BF16)
    ckv_ref[...] = (ckv * lax.rsqrt(jnp.mean(ckv * ckv, axis=-1, keepdims=True) + EPS) * gkv_ref[...]).astype(BF16)
    ang = pos_ref[...].astype(F32) * rc_ref[0:1, :]
    cz = jnp.cos(ang) * rc_ref[1:2, :]
    sz = jnp.sin(ang) * rc_ref[2:3, :]
    cz_ref[...] = cz
    sz_ref[...] = sz
    kpe_ref[...] = _rope_rot(down[:, Q_LORA + KV_LORA:], cz, sz).astype(BF16)


def _mla_down(x, g, sc, sh, w_ext, gq, gkv, pos_col, rope_c, tm=512):
    s, d = x.shape
    n = w_ext.shape[1]
    row = _full((1, d))
    return pl.pallas_call(
        _mla_down_body,
        out_shape=(
            jax.ShapeDtypeStruct((s, Q_LORA), BF16),
            jax.ShapeDtypeStruct((s, KV_LORA), BF16),
            jax.ShapeDtypeStruct((s, LANES), BF16),
            jax.ShapeDtypeStruct((s, LANES), F32),
            jax.ShapeDtypeStruct((s, LANES), F32),
        ),
        grid=(s // tm,),
        in_specs=[
            pl.BlockSpec((tm, d), lambda i: (i, 0)),
            row, row, row,
            _full((d, n)),
            _full((1, Q_LORA)),
            _full((1, KV_LORA)),
            pl.BlockSpec((tm, 1), lambda i: (i, 0)),
            _full((8, LANES)),
        ],
        out_specs=(
            pl.BlockSpec((tm, Q_LORA), lambda i: (i, 0)),
            pl.BlockSpec((tm, KV_LORA), lambda i: (i, 0)),
            pl.BlockSpec((tm, LANES), lambda i: (i, 0)),
            pl.BlockSpec((tm, LANES), lambda i: (i, 0)),
            pl.BlockSpec((tm, LANES), lambda i: (i, 0)),
        ),
        scratch_shapes=[pltpu.VMEM((tm, d), BF16), pltpu.VMEM((d, n), BF16)],
        compiler_params=_cparams(("arbitrary",), 56),
        name="mla_down",
    )(x, g, sc, sh, w_ext, gq, gkv, pos_col, rope_c)


def _mla_up_body(cq_ref, ckv_ref, kpe_ref, cz_ref, sz_ref, wq_ref, wkv_ref, q_ref, k_ref, v_ref, *, qscale):
    qf = jnp.dot(cq_ref[...], wq_ref[0].astype(BF16), preferred_element_type=F32)
    qpe = _rope_rot(qf[:, QK_NOPE_DIM:], cz_ref[...], sz_ref[...])
    q_ref[0] = (jnp.concatenate([qf[:, :QK_NOPE_DIM], qpe], axis=1) * qscale).astype(BF16)
    kv = jnp.dot(ckv_ref[...], wkv_ref[...].astype(BF16), preferred_element_type=F32)
    k_ref[0] = jnp.concatenate([kv[:, :QK_NOPE_DIM].astype(BF16), kpe_ref[...]], axis=1)
    v = kv[:, QK_NOPE_DIM:]
    v_ref[0] = jnp.concatenate([v, jnp.ones_like(v)], axis=1).astype(BF16)


def _mla_up(cq, ckv, kpe, cz, sz, wq_ext, wkv2d, tm=1024):
    s = cq.shape[0]
    qscale = (QK_HEAD_DIM ** -0.5) * math.log2(math.e)
    body = functools.partial(_mla_up_body, qscale=qscale)
    hs = jax.ShapeDtypeStruct((N_HEADS, s, HEAD_PAD), BF16)
    hspec = pl.BlockSpec((1, tm, HEAD_PAD), lambda i, h: (h, i, 0))
    return pl.pallas_call(
        body,
        out_shape=(hs, hs, hs),
        grid=(s // tm, N_HEADS),
        in_specs=[
            pl.BlockSpec((tm, Q_LORA), lambda i, h: (i, 0)),
            pl.BlockSpec((tm, KV_LORA), lambda i, h: (i, 0)),
            pl.BlockSpec((tm, LANES), lambda i, h: (i, 0)),
            pl.BlockSpec((tm, LANES), lambda i, h: (i, 0)),
            pl.BlockSpec((tm, LANES), lambda i, h: (i, 0)),
            pl.BlockSpec((1, Q_LORA, HEAD_PAD), lambda i, h: (h, 0, 0)),
            pl.BlockSpec((KV_LORA, HEAD_PAD), lambda i, h: (0, h)),
        ],
        out_specs=(hspec, hspec, hspec),
        compiler_params=_cparams(("arbitrary", "arbitrary"), 32),
        name="mla_up",
    )(cq, ckv, kpe, cz, sz, wq_ext, wkv2d)


ATTN_SUB_ROWS = 256


def _attn_body(q_ref, k_ref, v_ref, o_ref, m_scr, acc_scr, *, tk, nk):
    tq = q_ref.shape[1]
    m_scr[...] = jnp.full(m_scr.shape, -jnp.inf, F32)
    acc_scr[...] = jnp.zeros(acc_scr.shape, F32)

    def body(c, carry):
        r = pl.multiple_of(c * tk, tk)
        k = k_ref[0, pl.ds(r, tk), :]
        v = v_ref[0, pl.ds(r, tk), :]
        for r0 in range(0, tq, ATTN_SUB_ROWS):
            rows = slice(r0, r0 + ATTN_SUB_ROWS)
            s = lax.dot_general(q_ref[0, rows, :], k, (((1,), (1,)), ((), ())), preferred_element_type=F32)
            m_old = m_scr[rows, :]
            m_new = jnp.maximum(m_old, jnp.max(s, axis=-1, keepdims=True))
            alpha = jnp.exp2(m_old - m_new)
            p = jnp.exp2(s - m_new).astype(BF16)
            acc_scr[rows, :] = acc_scr[rows, :] * alpha + jnp.dot(p, v, preferred_element_type=F32)
            m_scr[rows, :] = m_new
        return carry

    lax.fori_loop(0, nk, body, 0, unroll=2 if nk % 2 == 0 else 1)
    acc = acc_scr[...]
    o_ref[...] = (acc[:, :V_HEAD_DIM] / acc[:, V_HEAD_DIM:V_HEAD_DIM + 1]).astype(o_ref.dtype)


def _attention(q, k, v, tq=2048, tk=1024):
    nh, s, _ = q.shape
    tk = min(tk, s)
    tq = min(tq, s)
    body = functools.partial(_attn_body, tk=tk, nk=s // tk)
    return pl.pallas_call(
        body,
        out_shape=jax.ShapeDtypeStruct((s, nh * V_HEAD_DIM), BF16),
        grid=(nh, s // tq),
        in_specs=[
            pl.BlockSpec((1, tq, HEAD_PAD), lambda h, i: (h, i, 0)),
            pl.BlockSpec((1, s, HEAD_PAD), lambda h, i: (h, 0, 0)),
            pl.BlockSpec((1, s, HEAD_PAD), lambda h, i: (h, 0, 0)),
        ],
        out_specs=pl.BlockSpec((tq, V_HEAD_DIM), lambda h, i: (i, h)),
        scratch_shapes=[pltpu.VMEM((tq, 1), F32), pltpu.VMEM((tq, HEAD_PAD), F32)],
        compiler_params=_cparams(("arbitrary", "arbitrary"), 40),
        name="attention",
    )(q, k, v)


def _dft_tables(s):
    n1 = DFT_N1
    n2 = s // n1
    m = FNET_GROUP_DIM
    c = np.arange(m, dtype=np.float64)
    ang = 2.0 * np.pi * np.outer(c, c) / m
    cs = np.concatenate([np.cos(ang), np.sin(ang)], axis=1)
    a1 = 2.0 * np.pi * np.outer(np.arange(n1), np.arange(n1)) / n1
    c1, s1 = np.cos(a1), np.sin(a1)
    m1 = np.block([[c1, -s1], [-s1, -c1]])
    at = 2.0 * np.pi * np.outer(np.arange(n2), np.arange(n1)) / s
    ct = np.cos(at)[:, :, None]
    st = np.sin(at)[:, :, None]
    a2 = 2.0 * np.pi * np.outer(np.arange(n2), np.arange(n2)) / n2
    scale = 1.0 / math.sqrt(float(s) * m)
    jb = 16
    gmat = np.zeros((n2, jb, 2, n2, jb), dtype=np.float64)
    for j in range(jb):
        gmat[:, j, 0, :, j] = np.cos(a2) * scale
        gmat[:, j, 1, :, j] = np.sin(a2) * scale
    gmat = gmat.reshape(n2 * jb, 2 * n2 * jb)
    return (jnp.asarray(cs, BF16), jnp.asarray(m1, BF16), jnp.asarray(ct, F32), jnp.asarray(st, F32),
            jnp.asarray(gmat, BF16))


def _fnet1_body(x_ref, g_ref, sc_ref, sh_ref, cs_ref, m1_ref, ct_ref, st_ref, o_ref, pq_scr, *, nb):
    n1 = DFT_N1
    g = g_ref[...]
    sp = 1.0 + sc_ref[...]
    sh = sh_ref[...]
    cs = cs_ref[...]
    m1 = m1_ref[...]
    gd = FNET_GROUP_DIM
    for j in range(nb):
        h = _norm_mod_rows(x_ref[:, j, :], g, sp, sh).astype(BF16)
        for gi in range(FNET_GROUPS):
            cols = slice(gi * gd, (gi + 1) * gd)
            pq = jnp.dot(h[:, cols], cs, preferred_element_type=F32)
            pq_scr[0:n1, cols] = pq[:, :gd].astype(BF16)
            pq_scr[n1:2 * n1, cols] = pq[:, gd:].astype(BF16)
        a = jnp.dot(m1, pq_scr[...], preferred_element_type=F32)
        ar = a[:n1]
        ai = a[n1:]
        ct = ct_ref[j]
        st = st_ref[j]
        o_ref[0, j] = (ar * ct + ai * st).astype(BF16)
        o_ref[1, j] = (ai * ct - ar * st).astype(BF16)


def _fnet_stage1(x, g, sc, sh, cs, m1, ct, st, nb=8):
    s, d = x.shape
    n1 = DFT_N1
    n2 = s // n1
    body = functools.partial(_fnet1_body, nb=nb)
    row = _full((1, d))
    return pl.pallas_call(
        body,
        out_shape=jax.ShapeDtypeStruct((2, n2, n1, d), BF16),
        grid=(n2 // nb,),
        in_specs=[
            pl.BlockSpec((n1, nb, d), lambda b: (0, b, 0)),
            row, row, row,
            _full(cs.shape),
            _full(m1.shape),
            pl.BlockSpec((nb, n1, 1), lambda b: (b, 0, 0)),
            pl.BlockSpec((nb, n1, 1), lambda b: (b, 0, 0)),
        ],
        out_specs=pl.BlockSpec((2, nb, n1, d), lambda b: (0, b, 0, 0)),
        scratch_shapes=[pltpu.VMEM((2 * n1, d), BF16)],
        compiler_params=_cparams(("arbitrary",), 48),
        name="fnet_stage1",
    )(x.reshape(n1, n2, d), g, sc, sh, cs, m1, ct, st)


def _fnet2_body(a_ref, g_ref, o_ref):
    blk = a_ref[...]
    rows = blk.shape[0] * blk.shape[1] * blk.shape[2]
    b2 = blk.reshape(rows, blk.shape[3])
    f = jnp.dot(g_ref[...], b2, preferred_element_type=F32)
    o_ref[...] = f.astype(o_ref.dtype).reshape(o_ref.shape)


def _fnet_stage2(a4, gmat, jb=16):
    _, n2, n1, d = a4.shape
    return pl.pallas_call(
        _fnet2_body,
        out_shape=jax.ShapeDtypeStruct((n2, n1, d), BF16),
        grid=(n1 // jb,),
        in_specs=[
            pl.BlockSpec((2, n2, jb, d), lambda b: (0, 0, b, 0)),
            _full(gmat.shape),
        ],
        out_specs=pl.BlockSpec((n2, jb, d), lambda b: (0, b, 0)),
        compiler_params=_cparams(("arbitrary",), 48),
        name="fnet_stage2",
    )(a4, gmat)


def _pool_body(xp_ref, x_ref, xn_ref, g_ref, sc_ref, sh_ref, w_ref, ps_ref, g1_ref, o_ref, hbuf, w_scr, *, ts, seq):
    i = pl.program_id(0)
    ni = pl.num_programs(0)
    halo = 8

    @pl.when(i == 0)
    def _():
        w_scr[...] = w_ref[...].astype(BF16)

    g = g_ref[...]
    sp = 1.0 + sc_ref[...]
    sh = sh_ref[...]
    hbuf[0:halo, :] = jnp.where(i > 0, _norm_mod_rows(xp_ref[...], g, sp, sh), 0.0)
    _norm_mod_store(x_ref, g_ref, sc_ref, sh_ref, hbuf.at[pl.ds(halo, ts), :], ts)
    hbuf[halo + ts:2 * halo + ts, :] = jnp.where(i < ni - 1, _norm_mod_rows(xn_ref[...], g, sp, sh), 0.0)

    t = i * ts + lax.broadcasted_iota(I32, (ts, 1), 0)
    gd = POOL_GROUP_DIM
    for gi, win in enumerate(POOL_WINDOWS):
        cols = slice(gi * gd, (gi + 1) * gd)
        half = win // 2
        wsum = hbuf[halo - half:halo - half + ts, cols]
        for off in range(-half + 1, win - half):
            wsum = wsum + hbuf[halo + off:halo + off + ts, cols]
        count = (jnp.minimum(t + (win - half), seq) - jnp.maximum(t - half, 0)).astype(F32)
        mixed = wsum / count - hbuf[halo:halo + ts, cols]
        y = jnp.dot(mixed.astype(BF16), w_scr[gi], preferred_element_type=F32)
        o_ref[:, cols] = x_ref[:, cols] + g1_ref[:, cols] * (y * ps_ref[:, cols])


def _pool_layer(x, g, sc, sh, w_pool, pscale, g1, ts=512):
    s, d = x.shape
    hb = ts // 8
    nhb = s // 8
    body = functools.partial(_pool_body, ts=ts, seq=s)
    row = _full((1, d))
    return pl.pallas_call(
        body,
        out_shape=jax.ShapeDtypeStruct((s, d), F32),
        grid=(s // ts,),
        in_specs=[
            pl.BlockSpec((8, d), lambda i: (jnp.maximum(i * hb - 1, 0), 0)),
            pl.BlockSpec((ts, d), lambda i: (i, 0)),
            pl.BlockSpec((8, d), lambda i: (jnp.minimum((i + 1) * hb, nhb - 1), 0)),
            row, row, row,
            _full(w_pool.shape),
            row, row,
        ],
        out_specs=pl.BlockSpec((ts, d), lambda i: (i, 0)),
        scratch_shapes=[pltpu.VMEM((ts + 16, d), F32), pltpu.VMEM(w_pool.shape, BF16)],
        compiler_params=_cparams(("arbitrary",), 48),
        name="pool_layer",
    )(x, x, x, g, sc, sh, w_pool, pscale, g1)


SLAB_X = D_MODEL // LANES
SLAB_ROWS = SLAB_X + 8


def _router_body(x_ref, g_ref, sc_ref, sh_ref, rw_ref, rb_ref, grp_ref, slab_ref, h_scr):
    tm = h_scr.shape[0]
    _norm_mod_store(x_ref, g_ref, sc_ref, sh_ref, h_scr, tm)
    h = h_scr[...]
    h_hi = h.astype(BF16)
    h_lo = (h - h_hi.astype(F32)).astype(BF16)
    rw = rw_ref[...]
    rw_hi = rw.astype(BF16)
    rw_lo = (rw - rw_hi.astype(F32)).astype(BF16)
    nt = (((1,), (1,)), ((), ()))
    p_hi = lax.dot_general(jnp.concatenate([rw_hi, rw_lo], axis=0), h_hi, nt, preferred_element_type=F32)
    p_lo = lax.dot_general(rw_hi, h_lo, nt, preferred_element_type=F32)
    logits = p_hi[:N_EXPERTS] + (p_hi[N_EXPERTS:] + p_lo)
    sc = jax.nn.sigmoid(logits)
    sel = sc + rb_ref[...]
    epg = EXPERTS_PER_GROUP
    rows = [sel[e:e + 1, :] for e in range(N_EXPERTS)]
    srow = [sc[e:e + 1, :] for e in range(N_EXPERTS)]
    gscore = []
    for gi in range(N_EXPERT_GROUPS):
        r = rows[gi * epg:(gi + 1) * epg]
        best = None
        for a in range(epg):
            for b in range(a + 1, epg):
                ps = r[a] + r[b]
                best = ps if best is None else jnp.maximum(best, ps)
        gscore.append(best)
    gidx = jnp.zeros_like(gscore[0], dtype=I32)
    gbest = gscore[0]
    for gi in range(1, N_EXPERT_GROUPS):
        better = gscore[gi] > gbest
        gidx = jnp.where(better, gi, gidx)
        gbest = jnp.where(better, gscore[gi], gbest)
    vin = []
    sin_ = []
    for j in range(epg):
        v = rows[j]
        sv = srow[j]
        for gi in range(1, N_EXPERT_GROUPS):
            pick = gidx == gi
            v = jnp.where(pick, rows[gi * epg + j], v)
            sv = jnp.where(pick, srow[gi * epg + j], sv)
        vin.append(v)
        sin_.append(sv)
    l1 = jnp.zeros_like(gidx)
    b1 = vin[0]
    for j in range(1, epg):
        better = vin[j] > b1
        l1 = jnp.where(better, j, l1)
        b1 = jnp.where(better, vin[j], b1)
    neg = jnp.full_like(b1, -jnp.inf)
    l2 = jnp.zeros_like(gidx)
    b2 = neg
    for j in range(epg):
        vj = jnp.where(l1 == j, neg, vin[j])
        better = vj > b2
        l2 = jnp.where(better, j, l2)
        b2 = jnp.where(better, vj, b2)
    w1 = sin_[0]
    w2 = sin_[0]
    for j in range(1, epg):
        w1 = jnp.where(l1 == j, sin_[j], w1)
        w2 = jnp.where(l2 == j, sin_[j], w2)
    tot = w1 + w2
    w1 = w1 / tot
    w2 = w2 / tot
    grp_ref[...] = gidx
    zero = jnp.zeros_like(w1)
    rid = lax.broadcasted_iota(I32, (LANES, tm), 0)
    info = jnp.zeros((LANES, tm), F32)
    for j in range(epg):
        cwj = jnp.where(l1 == j, w1, zero) + jnp.where(l2 == j, w2, zero)
        info = jnp.where(rid == j, cwj, info)
    slab_ref[:, 0:SLAB_X, :] = x_ref[...].reshape(tm, SLAB_X, LANES)
    tail = jnp.concatenate([info.T, jnp.zeros((tm, (SLAB_ROWS - SLAB_X - 1) * LANES), F32)], axis=1)
    slab_ref[:, SLAB_X:SLAB_ROWS, :] = tail.reshape(tm, SLAB_ROWS - SLAB_X, LANES)


def _router(x, g, sc, sh, rw_t, rb_col, tm=512):
    s, d = x.shape
    row = _full((1, d))
    return pl.pallas_call(
        _router_body,
        out_shape=(jax.ShapeDtypeStruct((1, s), I32), jax.ShapeDtypeStruct((s, SLAB_ROWS, LANES), F32)),
        grid=(s // tm,),
        in_specs=[
            pl.BlockSpec((tm, d), lambda i: (i, 0)),
            row, row, row,
            _full((N_EXPERTS, d)),
            _full((N_EXPERTS, 1)),
        ],
        out_specs=(pl.BlockSpec((1, tm), lambda i: (0, i)),
                   pl.BlockSpec((tm, SLAB_ROWS, LANES), lambda i: (i, 0, 0))),
        scratch_shapes=[pltpu.VMEM((tm, d), F32)],
        compiler_params=_cparams(("arbitrary",), 40),
        name="router",
    )(x, g, sc, sh, rw_t, rb_col)


def _invert_body(dest_ref, src_ref):
    n_slots = src_ref.shape[0]
    n_tok = dest_ref.shape[0]

    def clear(i, carry):
        src_ref[i] = 0
        return carry

    lax.fori_loop(0, n_slots, clear, 0, unroll=8)

    def fill(t, carry):
        src_ref[dest_ref[t]] = t
        return carry

    lax.fori_loop(0, n_tok, fill, 0, unroll=8)


def _invert(dest, n_slots):
    return pl.pallas_call(
        _invert_body,
        out_shape=jax.ShapeDtypeStruct((n_slots,), I32),
        grid_spec=pltpu.PrefetchScalarGridSpec(
            num_scalar_prefetch=1,
            grid=(1,),
            in_specs=[],
            out_specs=pl.BlockSpec(memory_space=pltpu.SMEM),
        ),
        compiler_params=_cparams(("arbitrary",), 16),
        name="moe_invert",
    )(dest)


def _slab_to_rows_body(x3_ref, o_ref):
    o_ref[...] = x3_ref[...].reshape(o_ref.shape)


def _slab_to_rows_norm_body(x3_ref, g_ref, o_ref):
    x = x3_ref[...].reshape(o_ref.shape)
    ms = jnp.mean(x * x, axis=-1, keepdims=True)
    o_ref[...] = (x * lax.rsqrt(ms + EPS)) * g_ref[...]


def _slab_to_rows(x3, s, norm_g=None, tm=512):
    d = SLAB_X * LANES
    in_specs = [pl.BlockSpec((tm, SLAB_X, LANES), lambda i: (i, 0, 0))]
    args = (x3,)
    if norm_g is not None:
        in_specs.append(_full((1, d)))
        args = (x3, norm_g)
    return pl.pallas_call(
        _slab_to_rows_body if norm_g is None else _slab_to_rows_norm_body,
        out_shape=jax.ShapeDtypeStruct((s, d), F32),
        grid=(s // tm,),
        in_specs=in_specs,
        out_specs=pl.BlockSpec((tm, d), lambda i: (i, 0)),
        compiler_params=_cparams(("arbitrary",), 32),
        name="slab_to_rows",
    )(*args)


def _moe_body(gid_ref, nu_ref, nv_ref, src_ref, slab_hbm, g_ref, sc_ref, sh_ref, g2_ref, wg_ref, wu_ref, wd_ref,
              out_hbm, xbuf, obuf, h_scr, acc_scr, act_scr, gsem, ssem, *, nslot):
    c = pl.program_id(0)
    sl = pl.program_id(1)
    nused = nu_ref[0]
    used = c < nused
    rows = h_scr.shape[0]
    per = rows // nslot
    buf = c % 2

    def gather_copy(chunk, r, b):
        tok = src_ref[chunk * rows + r]
        return pltpu.make_async_copy(slab_hbm.at[pl.ds(tok, 1)], xbuf.at[b, pl.ds(r, 1)], gsem.at[b])

    def start_gather(chunk, part, b):
        def body(i, carry):
            gather_copy(chunk, part * per + i, b).start()
            return carry

        lax.fori_loop(0, per, body, 0, unroll=8)

    n_tok = out_hbm.shape[0] - rows

    def scatter_copy(chunk, r):
        tok = jnp.where(r < nv_ref[chunk], src_ref[chunk * rows + r], n_tok + r)
        return pltpu.make_async_copy(obuf.at[pl.ds(r, 1)], out_hbm.at[pl.ds(tok, 1)], ssem.at[0])

    def wait_scatter():
        pltpu.make_async_copy(obuf, out_hbm.at[pl.ds(0, rows)], ssem.at[0]).wait()

    @pl.when(jnp.logical_and(c == 0, sl == 0))
    def _():
        for part in range(nslot):
            start_gather(0, part, 0)
        obuf[...] = jnp.zeros(obuf.shape, F32)
        init = pltpu.make_async_copy(obuf, out_hbm.at[pl.ds(n_tok, rows)], ssem.at[0])
        init.start()
        init.wait()

    @pl.when(jnp.logical_and(used, sl == 0))
    def _():
        pltpu.make_async_copy(slab_hbm.at[pl.ds(0, rows)], xbuf.at[buf], gsem.at[buf]).wait()

    @pl.when(c + 1 < nused)
    def _():
        start_gather(c + 1, sl, 1 - buf)

    @pl.when(jnp.logical_and(used, sl == 0))
    def _():
        g = g_ref[...]
        sp = 1.0 + sc_ref[...]
        sh = sh_ref[...]
        chunk = 32

        def body(k, carry):
            r = pl.multiple_of(k * chunk, chunk)
            x = xbuf[buf, pl.ds(r, chunk), 0:SLAB_X, :].reshape(chunk, SLAB_X * LANES)
            h_scr[pl.ds(r, chunk), :] = _norm_mod_rows(x, g, sp, sh).astype(BF16)
            return carry

        lax.fori_loop(0, rows // chunk, body, 0)
        acc_scr[...] = jnp.zeros(acc_scr.shape, F32)

    @pl.when(used)
    def _():
        h = h_scr[...]
        gate = jnp.dot(h, wg_ref[0, 0].astype(BF16), preferred_element_type=F32)
        up = jnp.dot(h, wu_ref[0, 0].astype(BF16), preferred_element_type=F32)
        j = sl // MOE_FF_SPLIT
        info = xbuf[buf, :, SLAB_X:SLAB_ROWS, :].reshape(rows, (SLAB_ROWS - SLAB_X) * LANES)[:, :LANES]
        cw = jnp.zeros((rows, 1), F32)
        for e in range(EXPERTS_PER_GROUP):
            cw = jnp.where(j == e, info[:, e:e + 1], cw)
        act = (gate * jax.nn.sigmoid(gate)) * up * cw
        part = sl % MOE_FF_SPLIT
        act_scr[part] = act.astype(BF16)

        @pl.when(part == MOE_FF_SPLIT - 1)
        def _():
            full = jnp.concatenate([act_scr[k] for k in range(MOE_FF_SPLIT)], axis=1)
            acc_scr[...] += jnp.dot(full, wd_ref[0, 0].astype(BF16), preferred_element_type=F32)

    @pl.when(jnp.logical_and(used, sl == nslot - 1))
    def _():
        @pl.when(c > 0)
        def _():
            wait_scatter()

        y = g2_ref[...] * acc_scr[...]
        obuf[...] = xbuf[buf, :, 0:SLAB_X, :] + y.reshape(rows, SLAB_X, LANES)

        def sbody(r, carry):
            scatter_copy(c, r).start()
            return carry

        lax.fori_loop(0, rows, sbody, 0, unroll=8)

        @pl.when(c == nused - 1)
        def _():
            wait_scatter()


def _moe_experts(gid, nused, nvalid, src, slabs, g, sc, sh, g2, w_gate, w_up, w_down, layer):
    s = slabs.shape[0]
    d = SLAB_X * LANES
    cm = MOE_CHUNK
    nc = src.shape[0] // cm
    nslot = EXPERTS_PER_GROUP * MOE_FF_SPLIT
    fs = EXPERT_FF // MOE_FF_SPLIT

    def eidx(c, sl, gid_ref, nu_ref):
        last = nu_ref[0] - 1
        cc = jnp.minimum(c, last)
        used = c <= last
        e = EXPERTS_PER_GROUP * gid_ref[cc] + jnp.where(used, sl // MOE_FF_SPLIT, EXPERTS_PER_GROUP - 1)
        half = jnp.where(used, sl % MOE_FF_SPLIT, MOE_FF_SPLIT - 1)
        return e, half

    def wgmap(c, sl, gid_ref, nu_ref, nv_ref, src_ref):
        e, half = eidx(c, sl, gid_ref, nu_ref)
        return (layer, e, 0, half)

    def wdmap(c, sl, gid_ref, nu_ref, nv_ref, src_ref):
        e, _ = eidx(c, sl, gid_ref, nu_ref)
        return (layer, e, 0, 0)

    row = pl.BlockSpec((1, d), lambda c, sl, *_: (0, 0))
    anyspec = pl.BlockSpec(memory_space=pl.ANY)
    body = functools.partial(_moe_body, nslot=nslot)
    return pl.pallas_call(
        body,
        out_shape=jax.ShapeDtypeStruct((s + cm, SLAB_X, LANES), F32),
        grid_spec=pltpu.PrefetchScalarGridSpec(
            num_scalar_prefetch=4,
            grid=(nc, nslot),
            in_specs=[
                anyspec,
                row, row, row, row,
                pl.BlockSpec((1, 1, d, fs), wgmap),
                pl.BlockSpec((1, 1, d, fs), wgmap),
                pl.BlockSpec((1, 1, EXPERT_FF, d), wdmap),
            ],
            out_specs=anyspec,
            scratch_shapes=[
                pltpu.VMEM((2, cm, SLAB_ROWS, LANES), F32),
                pltpu.VMEM((cm, SLAB_X, LANES), F32),
                pltpu.VMEM((cm, d), BF16),
                pltpu.VMEM((cm, d), F32),
                pltpu.VMEM((MOE_FF_SPLIT, cm, fs), BF16),
                pltpu.SemaphoreType.DMA((2,)),
                pltpu.SemaphoreType.DMA((1,)),
            ],
        ),
        compiler_params=_cparams(("arbitrary", "arbitrary"), 56),
        name="moe_experts",
    )(gid, nused, nvalid, src, slabs, g, sc, sh, g2, w_gate, w_up, w_down)


def _dispatch_tables(grp, nc):
    cm = MOE_CHUNK
    ng = N_EXPERT_GROUPS
    oh = (grp[:, None] == jnp.arange(ng, dtype=I32)[None, :]).astype(I32)
    cs = jnp.cumsum(oh, axis=0)
    counts = cs[-1]
    rank = jnp.sum(cs * oh, axis=1) - 1
    nch = (counts + cm - 1) // cm
    cum = jnp.cumsum(nch)
    first = cum - nch
    dest = jnp.sum(oh * (first * cm)[None, :], axis=1) + rank
    nused = cum[-1:]
    cidx = jnp.arange(nc, dtype=I32)
    gid = jnp.minimum(jnp.sum((cidx[:, None] >= cum[None, :]).astype(I32), axis=1), ng - 1)
    goh = (gid[:, None] == jnp.arange(ng, dtype=I32)[None, :]).astype(I32)
    left = jnp.sum(goh * counts[None, :], axis=1) - (cidx - jnp.sum(goh * first[None, :], axis=1)) * cm
    nvalid = jnp.where(cidx < cum[-1], jnp.clip(left, 0, cm), 0)
    return dest.astype(I32), gid.astype(I32), nused.astype(I32), nvalid.astype(I32)


def _moe_layer(x, g, sc, sh, g2, rw_t, rb_col, w_gate, w_up, w_down, layer, final_g=None):
    s, d = x.shape
    nc = s // MOE_CHUNK + N_EXPERT_GROUPS
    grp, slabs = _router(x, g, sc, sh, rw_t, rb_col)
    dest, gid, nused, nvalid = _dispatch_tables(grp[0], nc)
    src = _invert(dest, nc * MOE_CHUNK)
    y3 = _moe_experts(gid, nused, nvalid, src, slabs, g, sc, sh, g2, w_gate, w_up, w_down, layer)
    return _slab_to_rows(y3, s, final_g)


def _swap_halves(w):
    half = w.shape[-1] // 2
    return jnp.concatenate([w[..., half:], w[..., :half]], axis=-1)


def kernel(x, c, positions, ada_w, ada_b, norm_g, final_g, conv_w1, conv_b1, conv_wdw, conv_bdw, conv_ln_g, conv_ln_b, conv_w2, conv_b2, mla_w_down, mla_g_q, mla_g_kv, mla_w_uq, mla_w_ukv, mla_w_o, fnet_w, fnet_b, pool_w, pool_scale, router_w, router_b, moe_w_gate, moe_w_up, moe_w_down):
    b, s, d = x.shape
    assert b == 1 and d == D_MODEL
    xs = x.reshape(s, d)
    mod = _ada_mod(c, ada_w, ada_b).reshape(DEPTH, 6, 1, d)
    rw_t = router_w.T
    rb_col = router_b.reshape(N_EXPERTS, 1)
    zero_bias = jnp.zeros((1, d), F32)

    for i in range(DEPTH):
        sh1, sc1, g1, sh2, sc2, g2 = [mod[i, k] for k in range(6)]
        ng1 = norm_g[i, 0].reshape(1, d)
        ng2 = norm_g[i, 1].reshape(1, d)
        kind, j = i % 4, i // 4
        if kind == 0:
            u = _conv_glu(xs, ng1, sc1, sh1, conv_w1[j], conv_b1[j])
            dd = _dwconv_ln_silu(u, conv_wdw[j].reshape(CONV_WIDTH, d), conv_bdw[j].reshape(1, d),
                                 conv_ln_g[j].reshape(1, d), conv_ln_b[j].reshape(1, d))
            xs = _mm_res(dd, conv_w2[j], conv_b2[j].reshape(1, d), xs, g1)
        elif kind == 1:
            wd = mla_w_down[j]
            w_ext = jnp.concatenate([wd, _swap_halves(wd[:, Q_LORA + KV_LORA:])], axis=1)
            inv = ROPE_THETA ** (-jnp.arange(0, QK_ROPE_DIM, 2, dtype=F32) / QK_ROPE_DIM)
            zeros64 = jnp.zeros((QK_ROPE_DIM,), F32)
            half = QK_ROPE_DIM // 2
            rope_c = jnp.zeros((8, LANES), F32)
            rope_c = rope_c.at[0].set(jnp.concatenate([inv, inv, zeros64]))
            rope_c = rope_c.at[1].set(jnp.concatenate([jnp.ones((QK_ROPE_DIM,), F32), zeros64]))
            rope_c = rope_c.at[2].set(jnp.concatenate([-jnp.ones((half,), F32), jnp.ones((half,), F32), zeros64]))
            cq, ckv, kpe, cz, sz = _mla_down(xs, ng1, sc1, sh1, w_ext, mla_g_q[j].reshape(1, Q_LORA),
                                             mla_g_kv[j].reshape(1, KV_LORA), positions.reshape(s, 1), rope_c)
            wq = mla_w_uq[j]
            wq_ext = jnp.concatenate([wq, _swap_halves(wq[..., QK_NOPE_DIM:])], axis=-1)
            wq_ext = jnp.transpose(wq_ext, (1, 0, 2))
            wkv2d = mla_w_ukv[j].reshape(KV_LORA, N_HEADS * (QK_NOPE_DIM + V_HEAD_DIM))
            qh, kh, vh = _mla_up(cq, ckv, kpe, cz, sz, wq_ext, wkv2d)
            o = _attention(qh, kh, vh)
            xs = _mm_res(o, mla_w_o[j].reshape(N_HEADS * V_HEAD_DIM, d), zero_bias, xs, g1)
        elif kind == 2:
            cs, m1, ct, st, gmat = _dft_tables(s)
            a4 = _fnet_stage1(xs, ng1, sc1, sh1, cs, m1, ct, st)
            f = _fnet_stage2(a4, gmat)
            xs = _mm_res(f.reshape(s, d), fnet_w[j], fnet_b[j].reshape(1, d), xs, g1)
        else:
            xs = _pool_layer(xs, ng1, sc1, sh1, pool_w[j], pool_scale[j].reshape(1, d), g1)
        fin = final_g.reshape(1, d) if i == DEPTH - 1 else None
        xs = _moe_layer(xs, ng2, sc2, sh2, g2, rw_t, rb_col, moe_w_gate, moe_w_up, moe_w_down, i, fin)

    return xs.reshape(b, s, d)
```

```python
import functools
import math

import numpy as np
import jax
import jax.numpy as jnp
from jax import lax
from jax.experimental import pallas as pl
from jax.experimental.pallas import tpu as pltpu

F32 = jnp.float32
BF16 = jnp.bfloat16
I32 = jnp.int32

D_MODEL = 2048
DEPTH = 4
EPS = 1e-6
CONV_WIDTH = 31
CONV_PAD = CONV_WIDTH // 2
N_HEADS = 16
Q_LORA = 512
KV_LORA = 512
QK_NOPE_DIM = 128
QK_ROPE_DIM = 64
V_HEAD_DIM = 128
QK_HEAD_DIM = QK_NOPE_DIM + QK_ROPE_DIM
ROPE_THETA = 10000.0
FNET_GROUPS = 4
FNET_GROUP_DIM = D_MODEL // FNET_GROUPS
POOL_WINDOWS = (2, 4, 8, 16)
POOL_GROUP_DIM = D_MODEL // len(POOL_WINDOWS)
N_EXPERTS = 16
N_EXPERT_GROUPS = 4
EXPERTS_PER_GROUP = 4
EXPERT_FF = 512

LANES = 128
HEAD_PAD = 256
DFT_N1 = 128
MOE_CHUNK = 512
MOE_FF_SPLIT = 1
MIB = 1 << 20


def _cparams(sem, vmem_mib):
    return pltpu.CompilerParams(dimension_semantics=sem, vmem_limit_bytes=vmem_mib * MIB)


def _full(shape):
    nd = len(shape)
    return pl.BlockSpec(shape, lambda *_: (0,) * nd)


def _norm_mod_rows(x, g, sp, sh):
    ms = jnp.mean(x * x, axis=-1, keepdims=True)
    y = x * lax.rsqrt(ms + EPS)
    return (y * g) * sp + sh


def _norm_mod_store(x_ref, g_ref, sc_ref, sh_ref, out_ref, rows, chunk=32):
    g = g_ref[...]
    sp = 1.0 + sc_ref[...]
    sh = sh_ref[...]

    def body(c, carry):
        r = pl.multiple_of(c * chunk, chunk)
        x = x_ref[pl.ds(r, chunk), :]
        out_ref[pl.ds(r, chunk), :] = _norm_mod_rows(x, g, sp, sh).astype(out_ref.dtype)
        return carry

    lax.fori_loop(0, rows // chunk, body, 0)


def _mod_body(c_ref, w_ref, b_ref, o_ref):
    k_dim = c_ref.shape[0]
    tn = o_ref.shape[-1]
    rc = 8

    def body(k, acc):
        r = pl.multiple_of(k * rc, rc)
        c = c_ref[pl.ds(r, rc), :]
        ca = c * jax.nn.sigmoid(c)
        return acc + w_ref[0, pl.ds(r, rc), :] * ca

    acc = lax.fori_loop(0, k_dim // rc, body, jnp.zeros((rc, tn), F32), unroll=4)
    o_ref[0] = jnp.sum(acc, axis=0, keepdims=True) + b_ref[0]


def _ada_mod(c, ada_w, ada_b):
    depth, d, n = ada_w.shape
    tn = 2048
    return pl.pallas_call(
        _mod_body,
        out_shape=jax.ShapeDtypeStruct((depth, 1, n), F32),
        grid=(depth, n // tn),
        in_specs=[
            _full((d, 1)),
            pl.BlockSpec((1, d, tn), lambda i, j: (i, 0, j)),
            pl.BlockSpec((1, 1, tn), lambda i, j: (i, 0, j)),
        ],
        out_specs=pl.BlockSpec((1, 1, tn), lambda i, j: (i, 0, j)),
        compiler_params=_cparams(("arbitrary", "arbitrary"), 48),
        name="ada_mod",
    )(c.reshape(d, 1), ada_w, ada_b.reshape(depth, 1, n))


def _mm_res_body(a_ref, w_ref, b_ref, x_ref, g_ref, o_ref):
    y = jnp.dot(a_ref[...], w_ref[...].astype(BF16), preferred_element_type=F32) + b_ref[...]
    o_ref[...] = x_ref[...] + g_ref[...] * y


def _mm_res(a, w, b, x, gate, tm=1024, tn=512):
    m, k = a.shape
    n = w.shape[1]
    return pl.pallas_call(
        _mm_res_body,
        out_shape=jax.ShapeDtypeStruct((m, n), F32),
        grid=(m // tm, n // tn),
        in_specs=[
            pl.BlockSpec((tm, k), lambda i, j: (i, 0)),
            pl.BlockSpec((k, tn), lambda i, j: (0, j)),
            pl.BlockSpec((1, tn), lambda i, j: (0, j)),
            pl.BlockSpec((tm, tn), lambda i, j: (i, j)),
            pl.BlockSpec((1, tn), lambda i, j: (0, j)),
        ],
        out_specs=pl.BlockSpec((tm, tn), lambda i, j: (i, j)),
        compiler_params=_cparams(("arbitrary", "arbitrary"), 48),
        name="mm_res",
    )(a, w, b, x, gate)


def _conv1_body(x_ref, g_ref, sc_ref, sh_ref, wa_ref, wb_ref, ba_ref, bb_ref, u_ref, h_scr):
    @pl.when(pl.program_id(1) == 0)
    def _():
        _norm_mod_store(x_ref, g_ref, sc_ref, sh_ref, h_scr, h_scr.shape[0])

    h = h_scr[...]
    a = jnp.dot(h, wa_ref[...].astype(BF16), preferred_element_type=F32) + ba_ref[...]
    b = jnp.dot(h, wb_ref[...].astype(BF16), preferred_element_type=F32) + bb_ref[...]
    u_ref[...] = a * jax.nn.sigmoid(b)


def _conv_glu(x, g, sc, sh, w1, b1, tm=1024, tn=512):
    s, d = x.shape
    nb = d // tn
    b1r = b1.reshape(1, 2 * d)
    row = _full((1, d))
    return pl.pallas_call(
        _conv1_body,
        out_shape=jax.ShapeDtypeStruct((s, d), F32),
        grid=(s // tm, nb),
        in_specs=[
            pl.BlockSpec((tm, d), lambda i, j: (i, 0)),
            row, row, row,
            pl.BlockSpec((d, tn), lambda i, j: (0, j)),
            pl.BlockSpec((d, tn), lambda i, j: (0, j + nb)),
            pl.BlockSpec((1, tn), lambda i, j: (0, j)),
            pl.BlockSpec((1, tn), lambda i, j: (0, j + nb)),
        ],
        out_specs=pl.BlockSpec((tm, tn), lambda i, j: (i, j)),
        scratch_shapes=[pltpu.VMEM((tm, d), BF16)],
        compiler_params=_cparams(("arbitrary", "arbitrary"), 56),
        name="conv_glu",
    )(x, g, sc, sh, w1, w1, b1r, b1r)


def _dwconv_body(up_ref, u_ref, un_ref, w_ref, bdw_ref, lg_ref, lb_ref, d_ref, buf, cv, *, ts, cw, ncw):
    i = pl.program_id(0)
    cj = pl.program_id(1)
    ni = pl.num_programs(0)
    halo = 16
    buf[0:halo, :] = jnp.where(i > 0, up_ref[...], 0.0)
    buf[halo:halo + ts, :] = u_ref[...]
    buf[halo + ts:2 * halo + ts, :] = jnp.where(i < ni - 1, un_ref[...], 0.0)
    rc = 32
    for lc in range(cw // LANES):
        ls = slice(lc * LANES, (lc + 1) * LANES)
        wcol = w_ref[:, ls]
        bcol = bdw_ref[:, ls]

        def body(r, carry, ls=ls, wcol=wcol, bcol=bcol):
            r0 = pl.multiple_of(r * rc, rc)
            wrows = rc + 2 * halo
            win = buf[pl.ds(r0, wrows), ls]
            acc = jnp.zeros((rc, LANES), F32)
            for phase in range(8):
                shifted = win if phase == 0 else pltpu.roll(win, wrows - phase, 0)
                for k in range(CONV_WIDTH):
                    off = halo - CONV_PAD + k
                    if off % 8 == phase:
                        acc = acc + shifted[off - phase:off - phase + rc, :] * wcol[k:k + 1, :]
            cv[cj, pl.ds(r0, rc), ls] = acc + bcol
            return carry

        lax.fori_loop(0, ts // rc, body, 0)

    @pl.when(cj == ncw - 1)
    def _():
        d_model = ncw * cw
        rc2 = 32

        def body2(r, carry):
            r0 = pl.multiple_of(r * rc2, rc2)
            parts = [cv[c, pl.ds(r0, rc2), :] for c in range(ncw)]
            tot = parts[0].sum(axis=-1, keepdims=True)
            for p in parts[1:]:
                tot = tot + p.sum(axis=-1, keepdims=True)
            mu = tot / d_model
            cen = [p - mu for p in parts]
            sq = (cen[0] * cen[0]).sum(axis=-1, keepdims=True)
            for p in cen[1:]:
                sq = sq + (p * p).sum(axis=-1, keepdims=True)
            rinv = lax.rsqrt(sq / d_model + EPS)
            for c in range(ncw):
                cs = slice(c * cw, (c + 1) * cw)
                y = (cen[c] * rinv) * lg_ref[:, cs] + lb_ref[:, cs]
                d_ref[pl.ds(r0, rc2), cs] = (y * jax.nn.sigmoid(y)).astype(d_ref.dtype)
            return carry

        lax.fori_loop(0, ts // rc2, body2, 0)


def _dwconv_ln_silu(u, wdw, bdw, ln_g, ln_b, ts=512, cw=512):
    s, d = u.shape
    ncw = d // cw
    hb = ts // 16
    nhb = s // 16
    body = functools.partial(_dwconv_body, ts=ts, cw=cw, ncw=ncw)
    return pl.pallas_call(
        body,
        out_shape=jax.ShapeDtypeStruct((s, d), BF16),
        grid=(s // ts, ncw),
        in_specs=[
            pl.BlockSpec((16, cw), lambda i, j: (jnp.maximum(i * hb - 1, 0), j)),
            pl.BlockSpec((ts, cw), lambda i, j: (i, j)),
            pl.BlockSpec((16, cw), lambda i, j: (jnp.minimum((i + 1) * hb, nhb - 1), j)),
            pl.BlockSpec((CONV_WIDTH, cw), lambda i, j: (0, j)),
            pl.BlockSpec((1, cw), lambda i, j: (0, j)),
            _full((1, d)),
            _full((1, d)),
        ],
        out_specs=pl.BlockSpec((ts, d), lambda i, j: (i, 0)),
        scratch_shapes=[pltpu.VMEM((ts + 32, cw), F32), pltpu.VMEM((ncw, ts, cw), F32)],
        compiler_params=_cparams(("arbitrary", "arbitrary"), 32),
        name="dwconv_ln_silu",
    )(u, u, u, wdw, bdw, ln_g, ln_b)


def _rope_rot(t, cz, sz):
    return t * cz + pltpu.roll(t, QK_ROPE_DIM, 1) * sz


def _mla_down_body(x_ref, g_ref, sc_ref, sh_ref, w_ref, gq_ref, gkv_ref, pos_ref, rc_ref,
                   cq_ref, ckv_ref, kpe_ref, cz_ref, sz_ref, h_scr, w_scr):
    @pl.when(pl.program_id(0) == 0)
    def _():
        w_scr[...] = w_ref[...].astype(BF16)

    _norm_mod_store(x_ref, g_ref, sc_ref, sh_ref, h_scr, h_scr.shape[0])
    down = jnp.dot(h_scr[...], w_scr[...], preferred_element_type=F32)
    cq = down[:, :Q_LORA]
    ckv = down[:, Q_LORA:Q_LORA + KV_LORA]
    cq_ref[...] = (cq * lax.rsqrt(jnp.mean(cq * cq, axis=-1, keepdims=True) + EPS) * gq_ref[...]).astype(BF16)
    ckv_ref[...] = (ckv * lax.rsqrt(jnp.mean(ckv * ckv, axis=-1, keepdims=True) + EPS) * gkv_ref[...]).astype(BF16)
    ang = pos_ref[...].astype(F32) * rc_ref[0:1, :]
    cz = jnp.cos(ang) * rc_ref[1:2, :]
    sz = jnp.sin(ang) * rc_ref[2:3, :]
    cz_ref[...] = cz
    sz_ref[...] = sz
    kpe_ref[...] = _rope_rot(down[:, Q_LORA + KV_LORA:], cz, sz).astype(BF16)


def _mla_down(x, g, sc, sh, w_ext, gq, gkv, pos_col, rope_c, tm=512):
    s, d = x.shape
    n = w_ext.shape[1]
    row = _full((1, d))
    return pl.pallas_call(
        _mla_down_body,
        out_shape=(
            jax.ShapeDtypeStruct((s, Q_LORA), BF16),
            jax.ShapeDtypeStruct((s, KV_LORA), BF16),
            jax.ShapeDtypeStruct((s, LANES), BF16),
            jax.ShapeDtypeStruct((s, LANES), F32),
            jax.ShapeDtypeStruct((s, LANES), F32),
        ),
        grid=(s // tm,),
        in_specs=[
            pl.BlockSpec((tm, d), lambda i: (i, 0)),
            row, row, row,
            _full((d, n)),
            _full((1, Q_LORA)),
            _full((1, KV_LORA)),
            pl.BlockSpec((tm, 1), lambda i: (i, 0)),
            _full((8, LANES)),
        ],
        out_specs=(
            pl.BlockSpec((tm, Q_LORA), lambda i: (i, 0)),
            pl.BlockSpec((tm, KV_LORA), lambda i: (i, 0)),
            pl.BlockSpec((tm, LANES), lambda i: (i, 0)),
            pl.BlockSpec((tm, LANES), lambda i: (i, 0)),
            pl.BlockSpec((tm, LANES), lambda i: (i, 0)),
        ),
        scratch_shapes=[pltpu.VMEM((tm, d), BF16), pltpu.VMEM((d, n), BF16)],
        compiler_params=_cparams(("arbitrary",), 56),
        name="mla_down",
    )(x, g, sc, sh, w_ext, gq, gkv, pos_col, rope_c)


def _mla_up_body(cq_ref, ckv_ref, kpe_ref, cz_ref, sz_ref, wq_ref, wkv_ref, q_ref, k_ref, v_ref, *, qscale):
    qf = jnp.dot(cq_ref[...], wq_ref[0].astype(BF16), preferred_element_type=F32)
    qpe = _rope_rot(qf[:, QK_NOPE_DIM:], cz_ref[...], sz_ref[...])
    q_ref[0] = (jnp.concatenate([qf[:, :QK_NOPE_DIM], qpe], axis=1) * qscale).astype(BF16)
    kv = jnp.dot(ckv_ref[...], wkv_ref[...].astype(BF16), preferred_element_type=F32)
    k_ref[0] = jnp.concatenate([kv[:, :QK_NOPE_DIM].astype(BF16), kpe_ref[...]], axis=1)
    v = kv[:, QK_NOPE_DIM:]
    v_ref[0] = jnp.concatenate([v, jnp.ones_like(v)], axis=1).astype(BF16)


def _mla_up(cq, ckv, kpe, cz, sz, wq_ext, wkv2d, tm=1024):
    s = cq.shape[0]
    qscale = (QK_HEAD_DIM ** -0.5) * math.log2(math.e)
    body = functools.partial(_mla_up_body, qscale=qscale)
    hs = jax.ShapeDtypeStruct((N_HEADS, s, HEAD_PAD), BF16)
    hspec = pl.BlockSpec((1, tm, HEAD_PAD), lambda i, h: (h, i, 0))
    return pl.pallas_call(
        body,
        out_shape=(hs, hs, hs),
        grid=(s // tm, N_HEADS),
        in_specs=[
            pl.BlockSpec((tm, Q_LORA), lambda i, h: (i, 0)),
            pl.BlockSpec((tm, KV_LORA), lambda i, h: (i, 0)),
            pl.BlockSpec((tm, LANES), lambda i, h: (i, 0)),
            pl.BlockSpec((tm, LANES), lambda i, h: (i, 0)),
            pl.BlockSpec((tm, LANES), lambda i, h: (i, 0)),
            pl.BlockSpec((1, Q_LORA, HEAD_PAD), lambda i, h: (h, 0, 0)),
            pl.BlockSpec((KV_LORA, HEAD_PAD), lambda i, h: (0, h)),
        ],
        out_specs=(hspec, hspec, hspec),
        compiler_params=_cparams(("arbitrary", "arbitrary"), 32),
        name="mla_up",
    )(cq, ckv, kpe, cz, sz, wq_ext, wkv2d)


ATTN_SUB_ROWS = 256


def _attn_body(q_ref, k_ref, v_ref, o_ref, m_scr, acc_scr, *, tk, nk):
    tq = q_ref.shape[1]
    m_scr[...] = jnp.full(m_scr.shape, -jnp.inf, F32)
    acc_scr[...] = jnp.zeros(acc_scr.shape, F32)

    def body(c, carry):
        r = pl.multiple_of(c * tk, tk)
        k = k_ref[0, pl.ds(r, tk), :]
        v = v_ref[0, pl.ds(r, tk), :]
        for r0 in range(0, tq, ATTN_SUB_ROWS):
            rows = slice(r0, r0 + ATTN_SUB_ROWS)
            s = lax.dot_general(q_ref[0, rows, :], k, (((1,), (1,)), ((), ())), preferred_element_type=F32)
            m_old = m_scr[rows, :]
            m_new = jnp.maximum(m_old, jnp.max(s, axis=-1, keepdims=True))
            alpha = jnp.exp2(m_old - m_new)
            p = jnp.exp2(s - m_new).astype(BF16)
            acc_scr[rows, :] = acc_scr[rows, :] * alpha + jnp.dot(p, v, preferred_element_type=F32)
            m_scr[rows, :] = m_new
        return carry

    lax.fori_loop(0, nk, body, 0, unroll=4 if nk % 4 == 0 else 1)
    acc = acc_scr[...]
    o_ref[...] = (acc[:, :V_HEAD_DIM] / acc[:, V_HEAD_DIM:V_HEAD_DIM + 1]).astype(o_ref.dtype)


def _attention(q, k, v, tq=2048, tk=1024):
    nh, s, _ = q.shape
    tk = min(tk, s)
    tq = min(tq, s)
    body = functools.partial(_attn_body, tk=tk, nk=s // tk)
    return pl.pallas_call(
        body,
        out_shape=jax.ShapeDtypeStruct((s, nh * V_HEAD_DIM), BF16),
        grid=(nh, s // tq),
        in_specs=[
            pl.BlockSpec((1, tq, HEAD_PAD), lambda h, i: (h, i, 0)),
            pl.BlockSpec((1, s, HEAD_PAD), lambda h, i: (h, 0, 0)),
            pl.BlockSpec((1, s, HEAD_PAD), lambda h, i: (h, 0, 0)),
        ],
        out_specs=pl.BlockSpec((tq, V_HEAD_DIM), lambda h, i: (i, h)),
        scratch_shapes=[pltpu.VMEM((tq, 1), F32), pltpu.VMEM((tq, HEAD_PAD), F32)],
        compiler_params=_cparams(("arbitrary", "arbitrary"), 40),
        name="attention",
    )(q, k, v)


def _dft_tables(s):
    n1 = DFT_N1
    n2 = s // n1
    m = FNET_GROUP_DIM
    c = np.arange(m, dtype=np.float64)
    ang = 2.0 * np.pi * np.outer(c, c) / m
    cs = np.concatenate([np.cos(ang), np.sin(ang)], axis=1)
    a1 = 2.0 * np.pi * np.outer(np.arange(n1), np.arange(n1)) / n1
    c1, s1 = np.cos(a1), np.sin(a1)
    m1 = np.block([[c1, -s1], [-s1, -c1]])
    at = 2.0 * np.pi * np.outer(np.arange(n2), np.arange(n1)) / s
    ct = np.cos(at)[:, :, None]
    st = np.sin(at)[:, :, None]
    a2 = 2.0 * np.pi * np.outer(np.arange(n2), np.arange(n2)) / n2
    scale = 1.0 / math.sqrt(float(s) * m)
    jb = 16
    gmat = np.zeros((n2, jb, 2, n2, jb), dtype=np.float64)
    for j in range(jb):
        gmat[:, j, 0, :, j] = np.cos(a2) * scale
        gmat[:, j, 1, :, j] = np.sin(a2) * scale
    gmat = gmat.reshape(n2 * jb, 2 * n2 * jb)
    return (jnp.asarray(cs, BF16), jnp.asarray(m1, BF16), jnp.asarray(ct, F32), jnp.asarray(st, F32),
            jnp.asarray(gmat, BF16))


def _fnet1_body(x_ref, g_ref, sc_ref, sh_ref, cs_ref, m1_ref, ct_ref, st_ref, o_ref, pq_scr, *, nb):
    n1 = DFT_N1
    g = g_ref[...]
    sp = 1.0 + sc_ref[...]
    sh = sh_ref[...]
    cs = cs_ref[...]
    m1 = m1_ref[...]
    gd = FNET_GROUP_DIM
    for j in range(nb):
        h = _norm_mod_rows(x_ref[:, j, :], g, sp, sh).astype(BF16)
        for gi in range(FNET_GROUPS):
            cols = slice(gi * gd, (gi + 1) * gd)
            pq = jnp.dot(h[:, cols], cs, preferred_element_type=F32)
            pq_scr[0:n1, cols] = pq[:, :gd].astype(BF16)
            pq_scr[n1:2 * n1, cols] = pq[:, gd:].astype(BF16)
        a = jnp.dot(m1, pq_scr[...], preferred_element_type=F32)
        ar = a[:n1]
        ai = a[n1:]
        ct = ct_ref[j]
        st = st_ref[j]
        o_ref[0, j] = (ar * ct + ai * st).astype(BF16)
        o_ref[1, j] = (ai * ct - ar * st).astype(BF16)


def _fnet_stage1(x, g, sc, sh, cs, m1, ct, st, nb=8):
    s, d = x.shape
    n1 = DFT_N1
    n2 = s // n1
    body = functools.partial(_fnet1_body, nb=nb)
    row = _full((1, d))
    return pl.pallas_call(
        body,
        out_shape=jax.ShapeDtypeStruct((2, n2, n1, d), BF16),
        grid=(n2 // nb,),
        in_specs=[
            pl.BlockSpec((n1, nb, d), lambda b: (0, b, 0)),
            row, row, row,
            _full(cs.shape),
            _full(m1.shape),
            pl.BlockSpec((nb, n1, 1), lambda b: (b, 0, 0)),
            pl.BlockSpec((nb, n1, 1), lambda b: (b, 0, 0)),
        ],
        out_specs=pl.BlockSpec((2, nb, n1, d), lambda b: (0, b, 0, 0)),
        scratch_shapes=[pltpu.VMEM((2 * n1, d), BF16)],
        compiler_params=_cparams(("arbitrary",), 48),
        name="fnet_stage1",
    )(x.reshape(n1, n2, d), g, sc, sh, cs, m1, ct, st)


def _fnet2_body(a_ref, g_ref, o_ref):
    blk = a_ref[...]
    rows = blk.shape[0] * blk.shape[1] * blk.shape[2]
    b2 = blk.reshape(rows, blk.shape[3])
    f = jnp.dot(g_ref[...], b2, preferred_element_type=F32)
    o_ref[...] = f.astype(o_ref.dtype).reshape(o_ref.shape)


def _fnet_stage2(a4, gmat, jb=16):
    _, n2, n1, d = a4.shape
    return pl.pallas_call(
        _fnet2_body,
        out_shape=jax.ShapeDtypeStruct((n2, n1, d), BF16),
        grid=(n1 // jb,),
        in_specs=[
            pl.BlockSpec((2, n2, jb, d), lambda b: (0, 0, b, 0)),
            _full(gmat.shape),
        ],
        out_specs=pl.BlockSpec((n2, jb, d), lambda b: (0, b, 0)),
        compiler_params=_cparams(("arbitrary",), 48),
        name="fnet_stage2",
    )(a4, gmat)


def _pool_body(xp_ref, x_ref, xn_ref, g_ref, sc_ref, sh_ref, w_ref, ps_ref, g1_ref, o_ref, hbuf, w_scr, *, ts, seq):
    i = pl.program_id(0)
    ni = pl.num_programs(0)
    halo = 8

    @pl.when(i == 0)
    def _():
        w_scr[...] = w_ref[...].astype(BF16)

    g = g_ref[...]
    sp = 1.0 + sc_ref[...]
    sh = sh_ref[...]
    hbuf[0:halo, :] = jnp.where(i > 0, _norm_mod_rows(xp_ref[...], g, sp, sh), 0.0)
    _norm_mod_store(x_ref, g_ref, sc_ref, sh_ref, hbuf.at[pl.ds(halo, ts), :], ts)
    hbuf[halo + ts:2 * halo + ts, :] = jnp.where(i < ni - 1, _norm_mod_rows(xn_ref[...], g, sp, sh), 0.0)

    t = i * ts + lax.broadcasted_iota(I32, (ts, 1), 0)
    gd = POOL_GROUP_DIM
    for gi, win in enumerate(POOL_WINDOWS):
        cols = slice(gi * gd, (gi + 1) * gd)
        half = win // 2
        wsum = hbuf[halo - half:halo - half + ts, cols]
        for off in range(-half + 1, win - half):
            wsum = wsum + hbuf[halo + off:halo + off + ts, cols]
        count = (jnp.minimum(t + (win - half), seq) - jnp.maximum(t - half, 0)).astype(F32)
        mixed = wsum / count - hbuf[halo:halo + ts, cols]
        y = jnp.dot(mixed.astype(BF16), w_scr[gi], preferred_element_type=F32)
        o_ref[:, cols] = x_ref[:, cols] + g1_ref[:, cols] * (y * ps_ref[:, cols])


def _pool_layer(x, g, sc, sh, w_pool, pscale, g1, ts=512):
    s, d = x.shape
    hb = ts // 8
    nhb = s // 8
    body = functools.partial(_pool_body, ts=ts, seq=s)
    row = _full((1, d))
    return pl.pallas_call(
        body,
        out_shape=jax.ShapeDtypeStruct((s, d), F32),
        grid=(s // ts,),
        in_specs=[
            pl.BlockSpec((8, d), lambda i: (jnp.maximum(i * hb - 1, 0), 0)),
            pl.BlockSpec((ts, d), lambda i: (i, 0)),
            pl.BlockSpec((8, d), lambda i: (jnp.minimum((i + 1) * hb, nhb - 1), 0)),
            row, row, row,
            _full(w_pool.shape),
            row, row,
        ],
        out_specs=pl.BlockSpec((ts, d), lambda i: (i, 0)),
        scratch_shapes=[pltpu.VMEM((ts + 16, d), F32), pltpu.VMEM(w_pool.shape, BF16)],
        compiler_params=_cparams(("arbitrary",), 48),
        name="pool_layer",
    )(x, x, x, g, sc, sh, w_pool, pscale, g1)


SLAB_X = D_MODEL // LANES
SLAB_ROWS = SLAB_X + 8


def _router_body(x_ref, g_ref, sc_ref, sh_ref, rw_ref, rb_ref, grp_ref, slab_ref, h_scr):
    tm = h_scr.shape[0]
    _norm_mod_store(x_ref, g_ref, sc_ref, sh_ref, h_scr, tm)
    h = h_scr[...]
    h_hi = h.astype(BF16)
    h_lo = (h - h_hi.astype(F32)).astype(BF16)
    rw = rw_ref[...]
    rw_hi = rw.astype(BF16)
    rw_lo = (rw - rw_hi.astype(F32)).astype(BF16)
    nt = (((1,), (1,)), ((), ()))
    p_hi = lax.dot_general(jnp.concatenate([rw_hi, rw_lo], axis=0), h_hi, nt, preferred_element_type=F32)
    p_lo = lax.dot_general(rw_hi, h_lo, nt, preferred_element_type=F32)
    logits = p_hi[:N_EXPERTS] + (p_hi[N_EXPERTS:] + p_lo)
    sc = jax.nn.sigmoid(logits)
    sel = sc + rb_ref[...]
    epg = EXPERTS_PER_GROUP
    rows = [sel[e:e + 1, :] for e in range(N_EXPERTS)]
    srow = [sc[e:e + 1, :] for e in range(N_EXPERTS)]
    gscore = []
    for gi in range(N_EXPERT_GROUPS):
        r = rows[gi * epg:(gi + 1) * epg]
        best = None
        for a in range(epg):
            for b in range(a + 1, epg):
                ps = r[a] + r[b]
                best = ps if best is None else jnp.maximum(best, ps)
        gscore.append(best)
    gidx = jnp.zeros_like(gscore[0], dtype=I32)
    gbest = gscore[0]
    for gi in range(1, N_EXPERT_GROUPS):
        better = gscore[gi] > gbest
        gidx = jnp.where(better, gi, gidx)
        gbest = jnp.where(better, gscore[gi], gbest)
    vin = []
    sin_ = []
    for j in range(epg):
        v = rows[j]
        sv = srow[j]
        for gi in range(1, N_EXPERT_GROUPS):
            pick = gidx == gi
            v = jnp.where(pick, rows[gi * epg + j], v)
            sv = jnp.where(pick, srow[gi * epg + j], sv)
        vin.append(v)
        sin_.append(sv)
    l1 = jnp.zeros_like(gidx)
    b1 = vin[0]
    for j in range(1, epg):
        better = vin[j] > b1
        l1 = jnp.where(better, j, l1)
        b1 = jnp.where(better, vin[j], b1)
    neg = jnp.full_like(b1, -jnp.inf)
    l2 = jnp.zeros_like(gidx)
    b2 = neg
    for j in range(epg):
        vj = jnp.where(l1 == j, neg, vin[j])
        better = vj > b2
        l2 = jnp.where(better, j, l2)
        b2 = jnp.where(better, vj, b2)
    w1 = sin_[0]
    w2 = sin_[0]
    for j in range(1, epg):
        w1 = jnp.where(l1 == j, sin_[j], w1)
        w2 = jnp.where(l2 == j, sin_[j], w2)
    tot = w1 + w2
    w1 = w1 / tot
    w2 = w2 / tot
    grp_ref[...] = gidx
    zero = jnp.zeros_like(w1)
    rid = lax.broadcasted_iota(I32, (LANES, tm), 0)
    info = jnp.zeros((LANES, tm), F32)
    for j in range(epg):
        cwj = jnp.where(l1 == j, w1, zero) + jnp.where(l2 == j, w2, zero)
        info = jnp.where(rid == j, cwj, info)
    slab_ref[:, 0:SLAB_X, :] = x_ref[...].reshape(tm, SLAB_X, LANES)
    tail = jnp.concatenate([info.T, jnp.zeros((tm, (SLAB_ROWS - SLAB_X - 1) * LANES), F32)], axis=1)
    slab_ref[:, SLAB_X:SLAB_ROWS, :] = tail.reshape(tm, SLAB_ROWS - SLAB_X, LANES)


def _router(x, g, sc, sh, rw_t, rb_col, tm=512):
    s, d = x.shape
    row = _full((1, d))
    return pl.pallas_call(
        _router_body,
        out_shape=(jax.ShapeDtypeStruct((1, s), I32), jax.ShapeDtypeStruct((s, SLAB_ROWS, LANES), F32)),
        grid=(s // tm,),
        in_specs=[
            pl.BlockSpec((tm, d), lambda i: (i, 0)),
            row, row, row,
            _full((N_EXPERTS, d)),
            _full((N_EXPERTS, 1)),
        ],
        out_specs=(pl.BlockSpec((1, tm), lambda i: (0, i)),
                   pl.BlockSpec((tm, SLAB_ROWS, LANES), lambda i: (i, 0, 0))),
        scratch_shapes=[pltpu.VMEM((tm, d), F32)],
        compiler_params=_cparams(("arbitrary",), 40),
        name="router",
    )(x, g, sc, sh, rw_t, rb_col)


def _invert_body(dest_ref, src_ref):
    n_slots = src_ref.shape[0]
    n_tok = dest_ref.shape[0]

    def clear(i, carry):
        src_ref[i] = 0
        return carry

    lax.fori_loop(0, n_slots, clear, 0, unroll=8)

    def fill(t, carry):
        src_ref[dest_ref[t]] = t
        return carry

    lax.fori_loop(0, n_tok, fill, 0, unroll=8)


def _invert(dest, n_slots):
    return pl.pallas_call(
        _invert_body,
        out_shape=jax.ShapeDtypeStruct((n_slots,), I32),
        grid_spec=pltpu.PrefetchScalarGridSpec(
            num_scalar_prefetch=1,
            grid=(1,),
            in_specs=[],
            out_specs=pl.BlockSpec(memory_space=pltpu.SMEM),
        ),
        compiler_params=_cparams(("arbitrary",), 16),
        name="moe_invert",
    )(dest)


def _slab_to_rows_body(x3_ref, o_ref):
    o_ref[...] = x3_ref[...].reshape(o_ref.shape)


def _slab_to_rows_norm_body(x3_ref, g_ref, o_ref):
    x = x3_ref[...].reshape(o_ref.shape)
    ms = jnp.mean(x * x, axis=-1, keepdims=True)
    o_ref[...] = (x * lax.rsqrt(ms + EPS)) * g_ref[...]


def _slab_to_rows(x3, s, norm_g=None, tm=512):
    d = SLAB_X * LANES
    in_specs = [pl.BlockSpec((tm, SLAB_X, LANES), lambda i: (i, 0, 0))]
    args = (x3,)
    if norm_g is not None:
        in_specs.append(_full((1, d)))
        args = (x3, norm_g)
    return pl.pallas_call(
        _slab_to_rows_body if norm_g is None else _slab_to_rows_norm_body,
        out_shape=jax.ShapeDtypeStruct((s, d), F32),
        grid=(s // tm,),
        in_specs=in_specs,
        out_specs=pl.BlockSpec((tm, d), lambda i: (i, 0)),
        compiler_params=_cparams(("arbitrary",), 32),
        name="slab_to_rows",
    )(*args)


def _moe_body(gid_ref, nu_ref, nv_ref, src_ref, slab_hbm, g_ref, sc_ref, sh_ref, g2_ref, wg_ref, wu_ref, wd_ref,
              out_hbm, xbuf, obuf, h_scr, acc_scr, act_scr, gsem, ssem, *, nslot):
    c = pl.program_id(0)
    sl = pl.program_id(1)
    nused = nu_ref[0]
    used = c < nused
    rows = h_scr.shape[0]
    per = rows // nslot
    buf = c % 2

    def gather_copy(chunk, r, b):
        tok = src_ref[chunk * rows + r]
        return pltpu.make_async_copy(slab_hbm.at[pl.ds(tok, 1)], xbuf.at[b, pl.ds(r, 1)], gsem.at[b])

    def start_gather(chunk, part, b):
        def body(i, carry):
            gather_copy(chunk, part * per + i, b).start()
            return carry

        lax.fori_loop(0, per, body, 0, unroll=8)

    n_tok = out_hbm.shape[0] - rows

    def scatter_copy(chunk, r):
        tok = jnp.where(r < nv_ref[chunk], src_ref[chunk * rows + r], n_tok + r)
        return pltpu.make_async_copy(obuf.at[pl.ds(r, 1)], out_hbm.at[pl.ds(tok, 1)], ssem.at[0])

    def wait_scatter():
        pltpu.make_async_copy(obuf, out_hbm.at[pl.ds(0, rows)], ssem.at[0]).wait()

    @pl.when(jnp.logical_and(c == 0, sl == 0))
    def _():
        for part in range(nslot):
            start_gather(0, part, 0)
        obuf[...] = jnp.zeros(obuf.shape, F32)
        init = pltpu.make_async_copy(obuf, out_hbm.at[pl.ds(n_tok, rows)], ssem.at[0])
        init.start()
        init.wait()

    @pl.when(jnp.logical_and(used, sl == 0))
    def _():
        pltpu.make_async_copy(slab_hbm.at[pl.ds(0, rows)], xbuf.at[buf], gsem.at[buf]).wait()

    @pl.when(c + 1 < nused)
    def _():
        start_gather(c + 1, sl, 1 - buf)

    @pl.when(jnp.logical_and(used, sl == 0))
    def _():
        g = g_ref[...]
        sp = 1.0 + sc_ref[...]
        sh = sh_ref[...]
        chunk = 32

        def body(k, carry):
            r = pl.multiple_of(k * chunk, chunk)
            x = xbuf[buf, pl.ds(r, chunk), 0:SLAB_X, :].reshape(chunk, SLAB_X * LANES)
            h_scr[pl.ds(r, chunk), :] = _norm_mod_rows(x, g, sp, sh).astype(BF16)
            return carry

        lax.fori_loop(0, rows // chunk, body, 0)
        acc_scr[...] = jnp.zeros(acc_scr.shape, F32)

    @pl.when(used)
    def _():
        h = h_scr[...]
        gate = jnp.dot(h, wg_ref[0, 0].astype(BF16), preferred_element_type=F32)
        up = jnp.dot(h, wu_ref[0, 0].astype(BF16), preferred_element_type=F32)
        j = sl // MOE_FF_SPLIT
        info = xbuf[buf, :, SLAB_X:SLAB_ROWS, :].reshape(rows, (SLAB_ROWS - SLAB_X) * LANES)[:, :LANES]
        cw = jnp.zeros((rows, 1), F32)
        for e in range(EXPERTS_PER_GROUP):
            cw = jnp.where(j == e, info[:, e:e + 1], cw)
        act = (gate * jax.nn.sigmoid(gate)) * up * cw
        part = sl % MOE_FF_SPLIT
        act_scr[part] = act.astype(BF16)

        @pl.when(part == MOE_FF_SPLIT - 1)
        def _():
            full = jnp.concatenate([act_scr[k] for k in range(MOE_FF_SPLIT)], axis=1)
            acc_scr[...] += jnp.dot(full, wd_ref[0, 0].astype(BF16), preferred_element_type=F32)

    @pl.when(jnp.logical_and(used, sl == nslot - 1))
    def _():
        @pl.when(c > 0)
        def _():
            wait_scatter()

        y = g2_ref[...] * acc_scr[...]
        obuf[...] = xbuf[buf, :, 0:SLAB_X, :] + y.reshape(rows, SLAB_X, LANES)

        def sbody(r, carry):
            scatter_copy(c, r).start()
            return carry

        lax.fori_loop(0, rows, sbody, 0, unroll=8)

        @pl.when(c == nused - 1)
        def _():
            wait_scatter()


def _moe_experts(gid, nused, nvalid, src, slabs, g, sc, sh, g2, w_gate, w_up, w_down, layer):
    s = slabs.shape[0]
    d = SLAB_X * LANES
    cm = MOE_CHUNK
    nc = src.shape[0] // cm
    nslot = EXPERTS_PER_GROUP * MOE_FF_SPLIT
    fs = EXPERT_FF // MOE_FF_SPLIT

    def eidx(c, sl, gid_ref, nu_ref):
        last = nu_ref[0] - 1
        cc = jnp.minimum(c, last)
        used = c <= last
        e = EXPERTS_PER_GROUP * gid_ref[cc] + jnp.where(used, sl // MOE_FF_SPLIT, EXPERTS_PER_GROUP - 1)
        half = jnp.where(used, sl % MOE_FF_SPLIT, MOE_FF_SPLIT - 1)
        return e, half

    def wgmap(c, sl, gid_ref, nu_ref, nv_ref, src_ref):
        e, half = eidx(c, sl, gid_ref, nu_ref)
        return (layer, e, 0, half)

    def wdmap(c, sl, gid_ref, nu_ref, nv_ref, src_ref):
        e, _ = eidx(c, sl, gid_ref, nu_ref)
        return (layer, e, 0, 0)

    row = pl.BlockSpec((1, d), lambda c, sl, *_: (0, 0))
    anyspec = pl.BlockSpec(memory_space=pl.ANY)
    body = functools.partial(_moe_body, nslot=nslot)
    return pl.pallas_call(
        body,
        out_shape=jax.ShapeDtypeStruct((s + cm, SLAB_X, LANES), F32),
        grid_spec=pltpu.PrefetchScalarGridSpec(
            num_scalar_prefetch=4,
            grid=(nc, nslot),
            in_specs=[
                anyspec,
                row, row, row, row,
                pl.BlockSpec((1, 1, d, fs), wgmap),
                pl.BlockSpec((1, 1, d, fs), wgmap),
                pl.BlockSpec((1, 1, EXPERT_FF, d), wdmap),
            ],
            out_specs=anyspec,
            scratch_shapes=[
                pltpu.VMEM((2, cm, SLAB_ROWS, LANES), F32),
                pltpu.VMEM((cm, SLAB_X, LANES), F32),
                pltpu.VMEM((cm, d), BF16),
                pltpu.VMEM((cm, d), F32),
                pltpu.VMEM((MOE_FF_SPLIT, cm, fs), BF16),
                pltpu.SemaphoreType.DMA((2,)),
                pltpu.SemaphoreType.DMA((1,)),
            ],
        ),
        compiler_params=_cparams(("arbitrary", "arbitrary"), 56),
        name="moe_experts",
    )(gid, nused, nvalid, src, slabs, g, sc, sh, g2, w_gate, w_up, w_down)


def _dispatch_tables(grp, nc):
    cm = MOE_CHUNK
    ng = N_EXPERT_GROUPS
    oh = (grp[:, None] == jnp.arange(ng, dtype=I32)[None, :]).astype(I32)
    cs = jnp.cumsum(oh, axis=0)
    counts = cs[-1]
    rank = jnp.sum(cs * oh, axis=1) - 1
    nch = (counts + cm - 1) // cm
    cum = jnp.cumsum(nch)
    first = cum - nch
    dest = jnp.sum(oh * (first * cm)[None, :], axis=1) + rank
    nused = cum[-1:]
    cidx = jnp.arange(nc, dtype=I32)
    gid = jnp.minimum(jnp.sum((cidx[:, None] >= cum[None, :]).astype(I32), axis=1), ng - 1)
    goh = (gid[:, None] == jnp.arange(ng, dtype=I32)[None, :]).astype(I32)
    left = jnp.sum(goh * counts[None, :], axis=1) - (cidx - jnp.sum(goh * first[None, :], axis=1)) * cm
    nvalid = jnp.where(cidx < cum[-1], jnp.clip(left, 0, cm), 0)
    return dest.astype(I32), gid.astype(I32), nused.astype(I32), nvalid.astype(I32)


def _moe_layer(x, g, sc, sh, g2, rw_t, rb_col, w_gate, w_up, w_down, layer, final_g=None):
    s, d = x.shape
    nc = s // MOE_CHUNK + N_EXPERT_GROUPS
    grp, slabs = _router(x, g, sc, sh, rw_t, rb_col)
    dest, gid, nused, nvalid = _dispatch_tables(grp[0], nc)
    src = _invert(dest, nc * MOE_CHUNK)
    y3 = _moe_experts(gid, nused, nvalid, src, slabs, g, sc, sh, g2, w_gate, w_up, w_down, layer)
    return _slab_to_rows(y3, s, final_g)


def _swap_halves(w):
    half = w.shape[-1] // 2
    return jnp.concatenate([w[..., half:], w[..., :half]], axis=-1)


def kernel(x, c, positions, ada_w, ada_b, norm_g, final_g, conv_w1, conv_b1, conv_wdw, conv_bdw, conv_ln_g, conv_ln_b, conv_w2, conv_b2, mla_w_down, mla_g_q, mla_g_kv, mla_w_uq, mla_w_ukv, mla_w_o, fnet_w, fnet_b, pool_w, pool_scale, router_w, router_b, moe_w_gate, moe_w_up, moe_w_down):
    b, s, d = x.shape
    assert b == 1 and d == D_MODEL
    xs = x.reshape(s, d)
    mod = _ada_mod(c, ada_w, ada_b).reshape(DEPTH, 6, 1, d)
    rw_t = router_w.T
    rb_col = router_b.reshape(N_EXPERTS, 1)
    zero_bias = jnp.zeros((1, d), F32)

    for i in range(DEPTH):
        sh1, sc1, g1, sh2, sc2, g2 = [mod[i, k] for k in range(6)]
        ng1 = norm_g[i, 0].reshape(1, d)
        ng2 = norm_g[i, 1].reshape(1, d)
        kind, j = i % 4, i // 4
        if kind == 0:
            u = _conv_glu(xs, ng1, sc1, sh1, conv_w1[j], conv_b1[j])
            dd = _dwconv_ln_silu(u, conv_wdw[j].reshape(CONV_WIDTH, d), conv_bdw[j].reshape(1, d),
                                 conv_ln_g[j].reshape(1, d), conv_ln_b[j].reshape(1, d))
            xs = _mm_res(dd, conv_w2[j], conv_b2[j].reshape(1, d), xs, g1)
        elif kind == 1:
            wd = mla_w_down[j]
            w_ext = jnp.concatenate([wd, _swap_halves(wd[:, Q_LORA + KV_LORA:])], axis=1)
            inv = ROPE_THETA ** (-jnp.arange(0, QK_ROPE_DIM, 2, dtype=F32) / QK_ROPE_DIM)
            zeros64 = jnp.zeros((QK_ROPE_DIM,), F32)
            half = QK_ROPE_DIM // 2
            rope_c = jnp.zeros((8, LANES), F32)
            rope_c = rope_c.at[0].set(jnp.concatenate([inv, inv, zeros64]))
            rope_c = rope_c.at[1].set(jnp.concatenate([jnp.ones((QK_ROPE_DIM,), F32), zeros64]))
            rope_c = rope_c.at[2].set(jnp.concatenate([-jnp.ones((half,), F32), jnp.ones((half,), F32), zeros64]))
            cq, ckv, kpe, cz, sz = _mla_down(xs, ng1, sc1, sh1, w_ext, mla_g_q[j].reshape(1, Q_LORA),
                                             mla_g_kv[j].reshape(1, KV_LORA), positions.reshape(s, 1), rope_c)
            wq = mla_w_uq[j]
            wq_ext = jnp.concatenate([wq, _swap_halves(wq[..., QK_NOPE_DIM:])], axis=-1)
            wq_ext = jnp.transpose(wq_ext, (1, 0, 2))
            wkv2d = mla_w_ukv[j].reshape(KV_LORA, N_HEADS * (QK_NOPE_DIM + V_HEAD_DIM))
            qh, kh, vh = _mla_up(cq, ckv, kpe, cz, sz, wq_ext, wkv2d)
            o = _attention(qh, kh, vh)
            xs = _mm_res(o, mla_w_o[j].reshape(N_HEADS * V_HEAD_DIM, d), zero_bias, xs, g1)
        elif kind == 2:
            cs, m1, ct, st, gmat = _dft_tables(s)
            a4 = _fnet_stage1(xs, ng1, sc1, sh1, cs, m1, ct, st)
            f = _fnet_stage2(a4, gmat)
            xs = _mm_res(f.reshape(s, d), fnet_w[j], fnet_b[j].reshape(1, d), xs, g1)
        else:
            xs = _pool_layer(xs, ng1, sc1, sh1, pool_w[j], pool_scale[j].reshape(1, d), g1)
        fin = final_g.reshape(1, d) if i == DEPTH - 1 else None
        xs = _moe_layer(xs, ng2, sc2, sh2, g2, rw_t, rb_col, moe_w_gate, moe_w_up, moe_w_down, i, fin)

    return xs.reshape(b, s, d)
```
